```python
import jax, jax.numpy as jnp
from jax import lax
import numpy as np

D_MODEL = 1024
BATCH = 8
SEQ = 4096
DEPTH = 1
DEC_BATCH = 16
DEC_SEQ = 16
PAST_LEN = 1024

CHUNK = 64
Q_BLOCK = 128
EPS = 1e-6
ML_HEADS = 4
ML_DQK = 128
ML_DV = 256
ML_QK_W = ML_HEADS * ML_DQK
ML_V_W = ML_HEADS * ML_DV
MLA_HEADS = 8
MLA_DNOPE = 128
MLA_DROPE = 64
MLA_DV = 128
MLA_Q_RANK = 384
MLA_KV_RANK = 256
MLA_SCALE = (MLA_DNOPE + MLA_DROPE) ** -0.5
ROPE_THETA = 10000.0
IN_W = 2 * ML_QK_W + ML_V_W + 2 * ML_HEADS + ML_V_W + MLA_Q_RANK + MLA_KV_RANK + MLA_DROPE
N_EXPERTS = 32
TOP_K = 4
D_FF = 1024
SWIGLU_ALPHA = 1.702
SWIGLU_LIMIT = 7.0
MOE_BLOCK = 256

kernel_name = "hybrid_mlstm_mla_moe_stream_step"


def rmsnorm(x, g):
    xf = x.astype(jnp.float32)
    r = xf * lax.rsqrt(jnp.mean(xf * xf, axis=-1, keepdims=True) + EPS)
    return (r * g).astype(x.dtype)


def modulate(xn, shift, scale):
    return xn * (1.0 + scale[:, None, :]) + shift[:, None, :]


def rope_tables(pos):
    inv = 1.0 / (ROPE_THETA ** (jnp.arange(0, MLA_DROPE, 2, dtype=jnp.float32) / MLA_DROPE))
    ang = pos.astype(jnp.float32)[:, None] * inv[None, :]
    return jnp.cos(ang), jnp.sin(ang)


def apply_rope(x, cos, sin):
    x1, x2 = jnp.split(x.astype(jnp.float32), 2, axis=-1)
    return jnp.concatenate([x1 * cos - x2 * sin, x2 * cos + x1 * sin], axis=-1).astype(x.dtype)


def mixer_inputs(h, lw, pos):
    B, S, _ = h.shape
    sizes = [ML_QK_W, ML_QK_W, ML_V_W, ML_HEADS, ML_HEADS, ML_V_W, MLA_Q_RANK, MLA_KV_RANK, MLA_DROPE]
    points = [int(p) for p in np.cumsum(sizes)[:-1]]
    z = h @ lw["w_in"] + lw["b_in"]
    q_m, k_m, v_m, i_m, f_m, o_m, q_lat, kv_lat, k_pe = jnp.split(z, points, axis=-1)

    def heads(t, d):
        return t.reshape(B, S, ML_HEADS, d).transpose(0, 2, 1, 3).astype(jnp.float32)

    q = heads(q_m, ML_DQK)
    k = heads(k_m, ML_DQK) * (ML_DQK ** -0.5)
    v = heads(v_m, ML_DV)
    ig = i_m.astype(jnp.float32).transpose(0, 2, 1)
    lf = jax.nn.log_sigmoid(f_m.astype(jnp.float32) + lw["ml_f_bias"]).transpose(0, 2, 1)

    cos, sin = rope_tables(pos)
    cq = (rmsnorm(q_lat, lw["mla_g_q"]) @ lw["w_uq"]).reshape(B, S, MLA_HEADS, MLA_DNOPE + MLA_DROPE)
    qn, qr = jnp.split(cq, [MLA_DNOPE], axis=-1)
    qr = apply_rope(qr, cos[None, :, None, :], sin[None, :, None, :])
    latent = rmsnorm(kv_lat, lw["mla_g_kv"])
    kr = apply_rope(k_pe, cos[None], sin[None])
    return (q, k, v, ig, lf, o_m), (qn, qr, latent, kr)


def mlstm_chunk(carry, xs):
    C, n, m = carry
    q, k, v, ig, lf = xs
    L = q.shape[2]
    b = jnp.cumsum(lf, axis=-1)
    causal = jnp.tril(jnp.ones((L, L), dtype=bool))
    dmat = jnp.where(causal, b[..., :, None] - b[..., None, :] + ig[..., None, :], -jnp.inf)
    inter = b + m[..., None]
    m_t = jnp.maximum(inter, jnp.max(dmat, axis=-1))
    w_intra = jnp.exp(dmat - m_t[..., None])
    w_inter = jnp.exp(inter - m_t)
    qk = jnp.einsum('bhtd,bhsd->bhts', q, k) * w_intra
    num = w_inter[..., None] * jnp.einsum('bhtd,bhdv->bhtv', q, C) + jnp.einsum('bhts,bhsv->bhtv', qk, v)
    den = w_inter * jnp.einsum('bhtd,bhd->bht', q, n) + jnp.sum(qk, axis=-1)
    h = num / jnp.maximum(jnp.abs(den), jnp.exp(-m_t))[..., None]
    m_new = m_t[..., -1]
    w_prev = jnp.exp(inter[..., -1] - m_new)
    w_end = jnp.exp(b[..., -1:] - b + ig - m_new[..., None])
    C_new = w_prev[..., None, None] * C + jnp.einsum('bhs,bhsd,bhsv->bhdv', w_end, k, v)
    n_new = w_prev[..., None] * n + jnp.einsum('bhs,bhsd->bhd', w_end, k)
    return (C_new, n_new, m_new), h


def mlstm_output(hs, o_m, g_head):
    B, _, S, _ = hs.shape
    hn = hs * lax.rsqrt(jnp.mean(hs * hs, axis=-1, keepdims=True) + EPS)
    hn = hn.transpose(0, 2, 1, 3).reshape(B, S, ML_V_W)
    return (hn * g_head * jax.nn.sigmoid(o_m.astype(jnp.float32))).astype(o_m.dtype)


def mla_expand(latent, lw):
    B, K, _ = latent.shape
    kn = (latent @ lw["w_uk"]).reshape(B, K, MLA_HEADS, MLA_DNOPE)
    vv = (latent @ lw["w_uv"]).reshape(B, K, MLA_HEADS, MLA_DV)
    return kn, vv


def attend(qn, qr, kn, kr, vv, mask):
    s = jnp.einsum('bqhd,bkhd->bhqk', qn, kn) + jnp.einsum('bqhd,bkd->bhqk', qr, kr)
    s = jnp.where(mask, s.astype(jnp.float32) * MLA_SCALE, -jnp.inf)
    p = jax.nn.softmax(s, axis=-1).astype(vv.dtype)
    return jnp.einsum('bhqk,bkhd->bqhd', p, vv)


def mla_prompt_attention(qn, qr, kn, kr, vv):
    B, S = qn.shape[:2]
    nb = S // Q_BLOCK
    key_chunk = jnp.arange(S) // CHUNK

    def one_block(args):
        qn_b, qr_b, blk = args
        q_chunk = (blk * Q_BLOCK + jnp.arange(Q_BLOCK)) // CHUNK
        mask = key_chunk[None, :] <= q_chunk[:, None]
        return attend(qn_b, qr_b, kn, kr, vv, mask)

    qn_b = qn.reshape(B, nb, Q_BLOCK, MLA_HEADS, MLA_DNOPE).swapaxes(0, 1)
    qr_b = qr.reshape(B, nb, Q_BLOCK, MLA_HEADS, MLA_DROPE).swapaxes(0, 1)
    out = lax.map(one_block, (qn_b, qr_b, jnp.arange(nb)))
    return out.swapaxes(0, 1).reshape(B, S, MLA_HEADS * MLA_DV)


def moe_ffn(h, lw):
    lead = h.shape[:-1]
    t = h.reshape(-1, D_MODEL)
    n_tok = t.shape[0]
    logits = (t @ lw["w_router"] + lw["b_router"]).astype(jnp.float32)
    top_logit, top_e = lax.top_k(logits, TOP_K)
    top_w = jax.nn.softmax(top_logit, axis=-1)
    n_assign = n_tok * TOP_K
    e_flat = top_e.reshape(-1)
    tok_flat = jnp.repeat(jnp.arange(n_tok, dtype=jnp.int32), TOP_K)
    w_flat = top_w.reshape(-1)
    order = jnp.argsort(e_flat)
    e_sorted = e_flat[order]
    counts = jnp.bincount(e_flat, length=N_EXPERTS)
    padded = (counts + MOE_BLOCK - 1) // MOE_BLOCK * MOE_BLOCK
    pad_end = jnp.cumsum(padded)
    pad_start = pad_end - padded
    grp_start = jnp.cumsum(counts) - counts
    slot = pad_start[e_sorted] + jnp.arange(n_assign) - grp_start[e_sorted]
    n_blocks = -(-n_assign // MOE_BLOCK) + N_EXPERTS
    n_slots = n_blocks * MOE_BLOCK
    slot_tok = jnp.full((n_slots,), n_tok, jnp.int32).at[slot].set(tok_flat[order])
    slot_w = jnp.zeros((n_slots,), jnp.float32).at[slot].set(w_flat[order])
    block_e = jnp.minimum(jnp.searchsorted(pad_end, jnp.arange(n_blocks) * MOE_BLOCK, side='right'), N_EXPERTS - 1)
    t_pad = jnp.concatenate([t, jnp.zeros((1, D_MODEL), t.dtype)], axis=0)
    w_up, b_up, w_down, b_down = lw["w_up"], lw["b_up"], lw["w_down"], lw["b_down"]

    def run_block(args):
        tok, e = args
        gu = t_pad[tok] @ w_up[e] + b_up[e]
        gl, up = jnp.split(gu, 2, axis=-1)
        gl = jnp.minimum(gl, SWIGLU_LIMIT)
        up = jnp.clip(up, -SWIGLU_LIMIT, SWIGLU_LIMIT)
        act = (up + 1.0) * gl * jax.nn.sigmoid(SWIGLU_ALPHA * gl)
        return act @ w_down[e] + b_down[e]

    out = lax.map(run_block, (slot_tok.reshape(n_blocks, MOE_BLOCK), block_e))
    out = out.reshape(n_slots, D_MODEL) * slot_w[:, None].astype(out.dtype)
    y = jnp.zeros((n_tok + 1, D_MODEL), out.dtype).at[slot_tok].add(out)[:n_tok]
    return y.reshape(*lead, D_MODEL)


def ada_mod(c, lw):
    mod = c @ lw["w_ada"] + lw["b_ada"]
    return jnp.split(mod, 6, axis=-1)


def layer_tail(x, h, y_ml, y_mla, mods, lw):
    _, _, gt1, sh2, sc2, gt2 = mods
    g = jax.nn.sigmoid(h @ lw["w_gate"] + lw["b_gate"])
    g_ml, g_mla = jnp.split(g, 2, axis=-1)
    y = (g_ml * (y_ml @ lw["w_br_ml"]) + g_mla * (y_mla @ lw["w_br_mla"])) @ lw["w_out"]
    x = x + gt1[:, None, :] * rmsnorm(y, lw["g_post_mix"])
    h2 = modulate(rmsnorm(x, lw["g_pre_ffn"]), sh2, sc2)
    return x + gt2[:, None, :] * rmsnorm(moe_ffn(h2, lw), lw["g_post_ffn"])


def layer_prompt(x, c, lw):
    B, S, _ = x.shape
    mods = ada_mod(c, lw)
    h = modulate(rmsnorm(x, lw["g_pre_mix"]), mods[0], mods[1])
    (q, k, v, ig, lf, o_m), (qn, qr, latent, kr) = mixer_inputs(h, lw, jnp.arange(S))
    n_chunks = S // CHUNK

    def to_chunks(t):
        return jnp.moveaxis(t.reshape(t.shape[0], t.shape[1], n_chunks, CHUNK, *t.shape[3:]), 2, 0)

    init = (jnp.zeros((B, ML_HEADS, ML_DQK, ML_DV), jnp.float32),
            jnp.zeros((B, ML_HEADS, ML_DQK), jnp.float32),
            jnp.zeros((B, ML_HEADS), jnp.float32))
    (C, n, m), hs = lax.scan(mlstm_chunk, init, (to_chunks(q), to_chunks(k), to_chunks(v), to_chunks(ig), to_chunks(lf)))
    hs = jnp.moveaxis(hs, 0, 2).reshape(B, ML_HEADS, S, ML_DV)
    y_ml = mlstm_output(hs, o_m, lw["ml_head_g"])
    kn, vv = mla_expand(latent, lw)
    y_mla = mla_prompt_attention(qn, qr, kn, kr, vv)
    x = layer_tail(x, h, y_ml, y_mla, mods, lw)
    return x, (latent, kr, C.astype(x.dtype), n.astype(x.dtype), m.astype(x.dtype))


def layer_sample(x, c, cache_lat, cache_kr, C0, n0, m0, lw):
    B, S, _ = x.shape
    past = cache_lat.shape[1]
    mods = ada_mod(c, lw)
    h = modulate(rmsnorm(x, lw["g_pre_mix"]), mods[0], mods[1])
    (q, k, v, ig, lf, o_m), (qn, qr, latent, kr) = mixer_inputs(h, lw, past + jnp.arange(S))
    carry = (C0.astype(jnp.float32), n0.astype(jnp.float32), m0.astype(jnp.float32))
    (C, n, m), hs = mlstm_chunk(carry, (q, k, v, ig, lf))
    y_ml = mlstm_output(hs, o_m, lw["ml_head_g"])
    lat_all = jnp.concatenate([cache_lat.astype(latent.dtype), latent], axis=1)
    kr_all = jnp.concatenate([cache_kr.astype(kr.dtype), kr], axis=1)
    kn, vv = mla_expand(lat_all, lw)
    mask = jnp.ones((S, past + S), dtype=bool)
    y_mla = attend(qn, qr, kn, kr_all, vv, mask).reshape(B, S, MLA_HEADS * MLA_DV)
    x = layer_tail(x, h, y_ml, y_mla, mods, lw)
    return x, (latent, kr, C.astype(C0.dtype), n.astype(n0.dtype), m.astype(m0.dtype))


def setup_inputs(seed: int = 0) -> dict:
    key = jax.random.key(seed)
    ks = iter(jax.random.split(key, 40))

    def nrm(shape, scale):
        return jax.random.normal(next(ks), shape, jnp.float32) * scale

    L, D = DEPTH, D_MODEL
    HQ = MLA_HEADS * (MLA_DNOPE + MLA_DROPE)
    return {
        "x_prompt": nrm((BATCH, SEQ, D), 1.0),
        "x_sample": nrm((DEC_BATCH, DEC_SEQ, D), 1.0),
        "cache_kv_latent": nrm((L, DEC_BATCH, PAST_LEN, MLA_KV_RANK), 1.0),
        "cache_k_rope": nrm((L, DEC_BATCH, PAST_LEN, MLA_DROPE), 1.0),
        "state_mlstm_C": nrm((L, DEC_BATCH, ML_HEADS, ML_DQK, ML_DV), 0.1),
        "state_mlstm_n": nrm((L, DEC_BATCH, ML_HEADS, ML_DQK), 0.1),
        "state_mlstm_m": nrm((L, DEC_BATCH, ML_HEADS), 0.5),
        "c_prompt": nrm((BATCH, D), 1.0),
        "c_sample": nrm((DEC_BATCH, D), 1.0),
        "w_ada": nrm((L, D, 6 * D), 0.25 * D ** -0.5),
        "b_ada": nrm((L, 6 * D), 0.01),
        "g_pre_mix": 1.0 + nrm((L, D), 0.05),
        "g_post_mix": 1.0 + nrm((L, D), 0.05),
        "g_pre_ffn": 1.0 + nrm((L, D), 0.05),
        "g_post_ffn": 1.0 + nrm((L, D), 0.05),
        "w_in": nrm((L, D, IN_W), D ** -0.5),
        "b_in": nrm((L, IN_W), 0.01),
        "ml_f_bias": jnp.linspace(3.0, 6.0, ML_HEADS)[None, :] + nrm((L, ML_HEADS), 0.1),
        "ml_head_g": 1.0 + nrm((L, ML_V_W), 0.05),
        "mla_g_q": 1.0 + nrm((L, MLA_Q_RANK), 0.05),
        "w_uq": nrm((L, MLA_Q_RANK, HQ), MLA_Q_RANK ** -0.5),
        "mla_g_kv": 1.0 + nrm((L, MLA_KV_RANK), 0.05),
        "w_uk": nrm((L, MLA_KV_RANK, MLA_HEADS * MLA_DNOPE), MLA_KV_RANK ** -0.5),
        "w_uv": nrm((L, MLA_KV_RANK, MLA_HEADS * MLA_DV), MLA_KV_RANK ** -0.5),
        "w_br_ml": nrm((L, ML_V_W, D), ML_V_W ** -0.5),
        "w_br_mla": nrm((L, MLA_HEADS * MLA_DV, D), (MLA_HEADS * MLA_DV) ** -0.5),
        "w_gate": nrm((L, D, 2 * D), D ** -0.5),
        "b_gate": nrm((L, 2 * D), 0.01),
        "w_out": nrm((L, D, D), D ** -0.5),
        "w_router": nrm((L, D, N_EXPERTS), D ** -0.5),
        "b_router": nrm((L, N_EXPERTS), 0.01),
        "w_up": nrm((L, N_EXPERTS, D, 2 * D_FF), D ** -0.5),
        "b_up": nrm((L, N_EXPERTS, 2 * D_FF), 0.01),
        "w_down": nrm((L, N_EXPERTS, D_FF, D), D_FF ** -0.5),
        "b_down": nrm((L, N_EXPERTS, D), 0.01),
    }


def reference(x_prompt, x_sample, cache_kv_latent, cache_k_rope, state_mlstm_C, state_mlstm_n,
              state_mlstm_m, c_prompt, c_sample, w_ada, b_ada, g_pre_mix, g_post_mix, g_pre_ffn,
              g_post_ffn, w_in, b_in, ml_f_bias, ml_head_g, mla_g_q, w_uq, mla_g_kv, w_uk, w_uv,
              w_br_ml, w_br_mla, w_gate, b_gate, w_out, w_router, b_router, w_up, b_up, w_down, b_down):
    x_p, x_s = x_prompt, x_sample
    prompt_states, sample_states = [], []
    for l in range(DEPTH):
        lw = {
            "w_ada": w_ada[l], "b_ada": b_ada[l],
            "g_pre_mix": g_pre_mix[l], "g_post_mix": g_post_mix[l],
            "g_pre_ffn": g_pre_ffn[l], "g_post_ffn": g_post_ffn[l],
            "w_in": w_in[l], "b_in": b_in[l],
            "ml_f_bias": ml_f_bias[l], "ml_head_g": ml_head_g[l],
            "mla_g_q": mla_g_q[l], "w_uq": w_uq[l], "mla_g_kv": mla_g_kv[l],
            "w_uk": w_uk[l], "w_uv": w_uv[l],
            "w_br_ml": w_br_ml[l], "w_br_mla": w_br_mla[l],
            "w_gate": w_gate[l], "b_gate": b_gate[l], "w_out": w_out[l],
            "w_router": w_router[l], "b_router": b_router[l],
            "w_up": w_up[l], "b_up": b_up[l], "w_down": w_down[l], "b_down": b_down[l],
        }
        x_p, st_p = layer_prompt(x_p, c_prompt, lw)
        x_s, st_s = layer_sample(x_s, c_sample, cache_kv_latent[l], cache_k_rope[l],
                                 state_mlstm_C[l], state_mlstm_n[l], state_mlstm_m[l], lw)
        prompt_states.append(st_p)
        sample_states.append(st_s)
    lat_p, kr_p, C_p, n_p, m_p = (jnp.stack(t, axis=0) for t in zip(*prompt_states))
    lat_s, kr_s, C_s, n_s, m_s = (jnp.stack(t, axis=0) for t in zip(*sample_states))
    return (x_p, x_s, lat_p, kr_p, C_p, n_p, m_p, lat_s, kr_s, C_s, n_s, m_s)
```

```python
import functools

import jax
import jax.numpy as jnp
import numpy as np
from jax import lax
from jax.experimental import pallas as pl
from jax.experimental.pallas import tpu as pltpu

F32 = jnp.float32
BF16 = jnp.bfloat16

EPS = 1e-6
ROPE_THETA = 10000.0
SWIGLU_ALPHA = 1.702
SWIGLU_LIMIT = 7.0
TOP_K = 4

ML_HEADS = 4
ML_DQK = 128
ML_DV = 256
MLA_HEADS = 8
MLA_DNOPE = 128
MLA_DROPE = 64
MLA_DV = 128
MLA_Q_RANK = 384
MLA_KV_RANK = 256
MLA_SCALE = (MLA_DNOPE + MLA_DROPE) ** -0.5
QK_PAD = 256
N_EXPERTS = 32

VMEM_LIMIT_BYTES = 52 * 1024 * 1024

TOK_TILE = 256
ML_CHUNK = 256
ATT_TILE = 512
STREAM_CHUNK = 64
MOE_ROWS = 512
ORDER_BITS = 18


def _cparams(sem, vmem=VMEM_LIMIT_BYTES):
    return pltpu.CompilerParams(dimension_semantics=sem, vmem_limit_bytes=vmem)


def _dot(a, b):
    return jnp.dot(a, b, preferred_element_type=F32)


def _dot_nt(a, b):
    return lax.dot_general(a, b, (((1,), (1,)), ((), ())), preferred_element_type=F32)


def _rms(x, g):
    return x * lax.rsqrt(jnp.mean(x * x, axis=-1, keepdims=True) + EPS) * g


def _split3(x):
    hi = x.astype(BF16)
    r1 = x - hi.astype(F32)
    mid = r1.astype(BF16)
    lo = (r1 - mid.astype(F32)).astype(BF16)
    return hi, mid, lo


def _ada_body(c_ref, w_ref, b_ref, o_ref):
    o_ref[...] = _dot(c_ref[...].astype(BF16), w_ref[...].astype(BF16)) + b_ref[...]


def _ada(c, w, b):
    m, d = c.shape
    n = w.shape[1]
    return pl.pallas_call(
        _ada_body,
        grid=(n // d,),
        in_specs=[pl.BlockSpec((m, d), lambda j: (0, 0)),
                  pl.BlockSpec((d, d), lambda j: (0, j)),
                  pl.BlockSpec((1, d), lambda j: (0, j))],
        out_specs=pl.BlockSpec((m, d), lambda j: (0, j)),
        out_shape=jax.ShapeDtypeStruct((m, n), F32),
        compiler_params=_cparams(("parallel",)),
    )(c, w, b)


_SM_QLAT = 0
_SM_KVLAT = MLA_Q_RANK
_SM_KPE = MLA_Q_RANK + MLA_KV_RANK
_SM_GATE = _SM_KPE + 128
_SM_W = _SM_GATE + 128


def _mixer_in_body(x_ref, mod_ref, gpre_ref, wm_ref, bm_ref, ws_ref, bs_ref, fb_ref, gq_ref, gkv_ref,
                   wuq_ref, wuk_ref, wuv_ref, cs_ref,
                   q_ref, k_ref, v_ref, o_ref, gate_ref, lat_ref, kr_ref, qcat_ref, kcat_ref, vv_ref):
    x = x_ref[...]
    shift = mod_ref[0, 0]
    scale = mod_ref[1, 0]
    h = _rms(x, gpre_ref[...]) * (1.0 + scale) + shift
    hb = h.astype(BF16)
    qk_w = ML_HEADS * ML_DQK
    v_w = ML_HEADS * ML_DV
    q_ref[...] = (_dot(hb, wm_ref[:, 0:qk_w]) + bm_ref[:, 0:qk_w]).astype(BF16)
    k_ref[...] = ((_dot(hb, wm_ref[:, qk_w:2 * qk_w]) + bm_ref[:, qk_w:2 * qk_w]) * (ML_DQK ** -0.5)).astype(BF16)
    v_ref[...] = (_dot(hb, wm_ref[:, 2 * qk_w:2 * qk_w + v_w]) + bm_ref[:, 2 * qk_w:2 * qk_w + v_w]).astype(BF16)
    o_ref[...] = (_dot(hb, wm_ref[:, 2 * qk_w + v_w:]) + bm_ref[:, 2 * qk_w + v_w:]).astype(BF16)

    zs = _dot(hb, ws_ref[...]) + bs_ref[...]
    g = zs[:, _SM_GATE:_SM_GATE + 128]
    gf = g + fb_ref[...]
    logsig = jnp.minimum(gf, 0.0) - jnp.log1p(jnp.exp(-jnp.abs(gf)))
    lane = lax.broadcasted_iota(jnp.int32, g.shape, 1)
    gate_ref[...] = jnp.where(lane >= ML_HEADS, logsig, g)

    cs = cs_ref[...]
    qln = _rms(zs[:, _SM_QLAT:_SM_QLAT + MLA_Q_RANK], gq_ref[...]).astype(BF16)
    cq = _dot(qln, wuq_ref[...])
    for hd in range(MLA_HEADS):
        blk = cq[:, hd * QK_PAD:(hd + 1) * QK_PAD]
        t = blk[:, MLA_DNOPE:] * cs
        rot = t + pltpu.roll(t, 64, 1)
        qcat_ref[hd, :, 0:MLA_DNOPE] = (blk[:, :MLA_DNOPE] * MLA_SCALE).astype(BF16)
        qcat_ref[hd, :, MLA_DNOPE:] = (rot * MLA_SCALE).astype(BF16)
    latent = _rms(zs[:, _SM_KVLAT:_SM_KVLAT + MLA_KV_RANK], gkv_ref[...])
    lat_ref[...] = latent
    tk = zs[:, _SM_KPE:_SM_KPE + 128] * cs
    rotk = tk + pltpu.roll(tk, 64, 1)
    kr_ref[...] = rotk[:, 0:MLA_DROPE]
    krz = jnp.where(lane < MLA_DROPE, rotk, 0.0).astype(BF16)
    latb = latent.astype(BF16)
    kn = _dot(latb, wuk_ref[...])
    vv = _dot(latb, wuv_ref[...])
    for hd in range(MLA_HEADS):
        kcat_ref[hd, :, 0:MLA_DNOPE] = kn[:, hd * MLA_DNOPE:(hd + 1) * MLA_DNOPE].astype(BF16)
        kcat_ref[hd, :, MLA_DNOPE:] = krz
        vv_ref[hd] = vv[:, hd * MLA_DV:(hd + 1) * MLA_DV].astype(BF16)


def _mixer_in(x, mods, cs, w, tile, tiles_per_mod, cs_tiles):
    n, d = x.shape
    nm, _, r, _ = mods.shape
    grid = (n // tile,)
    const = lambda a: pl.BlockSpec(a.shape, lambda i: (0,) * a.ndim)
    tok = lambda width: pl.BlockSpec((tile, width), lambda i: (i, 0))
    head = lambda width: pl.BlockSpec((MLA_HEADS, tile, width), lambda i: (0, i, 0))
    qk_w = ML_HEADS * ML_DQK
    v_w = ML_HEADS * ML_DV
    out_shape = (
        jax.ShapeDtypeStruct((n, qk_w), BF16), jax.ShapeDtypeStruct((n, qk_w), BF16),
        jax.ShapeDtypeStruct((n, v_w), BF16), jax.ShapeDtypeStruct((n, v_w), BF16),
        jax.ShapeDtypeStruct((n, 128), F32),
        jax.ShapeDtypeStruct((n, MLA_KV_RANK), F32), jax.ShapeDtypeStruct((n, MLA_DROPE), F32),
        jax.ShapeDtypeStruct((MLA_HEADS, n, QK_PAD), BF16), jax.ShapeDtypeStruct((MLA_HEADS, n, QK_PAD), BF16),
        jax.ShapeDtypeStruct((MLA_HEADS, n, MLA_DV), BF16),
    )
    out_specs = (tok(qk_w), tok(qk_w), tok(v_w), tok(v_w), tok(128), tok(MLA_KV_RANK), tok(MLA_DROPE),
                 head(QK_PAD), head(QK_PAD), head(MLA_DV))
    in_specs = [
        tok(d),
        pl.BlockSpec((nm, 1, r, d), lambda i: (0, i // tiles_per_mod, 0, 0)),
        const(w["g_pre_mix"]), const(w["w_main"]), const(w["b_main"]), const(w["w_small"]), const(w["b_small"]),
        const(w["f_bias"]), const(w["g_q"]), const(w["g_kv"]), const(w["w_uq"]), const(w["w_uk"]), const(w["w_uv"]),
        pl.BlockSpec((tile, 128), lambda i: (i % cs_tiles, 0)),
    ]
    return pl.pallas_call(
        _mixer_in_body, grid=grid, in_specs=in_specs, out_specs=out_specs, out_shape=out_shape,
        compiler_params=_cparams(("parallel",)),
    )(x, mods, w["g_pre_mix"], w["w_main"], w["b_main"], w["w_small"], w["b_small"], w["f_bias"],
      w["g_q"], w["g_kv"], w["w_uq"], w["w_uk"], w["w_uv"], cs)


def _mlstm_body(q_ref, k_ref, v_ref, o_ref, gc_ref, gr_ref, c0_ref, n0_ref, m0_ref, gh_ref,
                y_ref, c_ref, n_ref, m_ref, br_ref):
    chunk = q_ref.shape[0]

    @pl.when(pl.program_id(1) == 0)
    def _():
        c_ref[...] = c0_ref[...]
        n_ref[...] = n0_ref[...]
        m_ref[...] = m0_ref[...]

    row = lax.broadcasted_iota(jnp.int32, (chunk, chunk), 0)
    col = lax.broadcasted_iota(jnp.int32, (chunk, chunk), 1)
    causal = col <= row
    tri = causal.astype(BF16)
    tri_t = (row <= col).astype(BF16)

    gc = gc_ref[...]
    gr = gr_ref[0]
    bc_all = sum(_dot(tri, p) for p in _split3(gc))
    br_ref[...] = sum(_dot(p, tri_t) for p in _split3(gr))

    lane = lax.broadcasted_iota(jnp.int32, gc.shape, 1)
    rowv = lax.broadcasted_iota(jnp.int32, (chunk, 1), 0)
    is_last = rowv == chunk - 1

    def pick(a, j):
        return jnp.sum(jnp.where(lane == j, a, 0.0), axis=-1, keepdims=True)

    def last(a):
        return jnp.sum(jnp.where(is_last, a, 0.0), axis=0, keepdims=True)

    for hd in range(ML_HEADS):
        qs = slice(hd * ML_DQK, (hd + 1) * ML_DQK)
        vs = slice(hd * ML_DV, (hd + 1) * ML_DV)
        q = q_ref[:, qs]
        k = k_ref[:, qs]
        v = v_ref[:, vs]
        ig_c = pick(gc, hd)
        b_c = pick(bc_all, ML_HEADS + hd)
        ig_r = gr_ref[0, hd:hd + 1, :]
        b_r = br_ref[ML_HEADS + hd:ML_HEADS + hd + 1, :]
        m_prev = m_ref[0, hd]
        c_prev = c_ref[0, hd]
        n_prev = n_ref[0, hd]

        dmat = jnp.where(causal, b_c - b_r + ig_r, -jnp.inf)
        inter = b_c + m_prev
        m_t = jnp.maximum(inter, jnp.max(dmat, axis=-1, keepdims=True))
        w_intra = jnp.exp(dmat - m_t)
        w_inter = jnp.exp(inter - m_t)
        qk = _dot_nt(q, k) * w_intra
        num = w_inter * _dot(q, c_prev.astype(BF16)) + _dot(qk.astype(BF16), v)
        qn = jnp.sum(q.astype(F32) * n_prev, axis=-1, keepdims=True)
        den = w_inter * qn + jnp.sum(qk, axis=-1, keepdims=True)
        hcur = num / jnp.maximum(jnp.abs(den), jnp.exp(-m_t))
        hn = hcur * lax.rsqrt(jnp.mean(hcur * hcur, axis=-1, keepdims=True) + EPS)
        y_ref[:, vs] = (hn * gh_ref[:, vs] * jax.nn.sigmoid(o_ref[:, vs].astype(F32))).astype(BF16)

        m_new = last(m_t)
        w_prev = jnp.exp(last(inter) - m_new)
        w_end = jnp.exp(last(b_c) - b_c + ig_c - m_new)
        kw = k.astype(F32) * w_end
        c_ref[0, hd] = w_prev * c_prev + _dot(kw.T.astype(BF16), v)
        n_ref[0, hd] = w_prev * n_prev + jnp.sum(kw, axis=0, keepdims=True)
        m_ref[0, hd] = m_new


def _mlstm(q, k, v, o, gates_col, gates_row, c0, n0, m0, g_head, chunk):
    n = q.shape[0]
    b, _, s = gates_row.shape
    nc = s // chunk
    tok = lambda width: pl.BlockSpec((chunk, width), lambda i, c: (i * nc + c, 0))
    state = lambda a: pl.BlockSpec((1,) + a.shape[1:], lambda i, c: (i,) + (0,) * (a.ndim - 1))
    qk_w = ML_HEADS * ML_DQK
    v_w = ML_HEADS * ML_DV
    return pl.pallas_call(
        _mlstm_body,
        grid=(b, nc),
        in_specs=[tok(qk_w), tok(qk_w), tok(v_w), tok(v_w), tok(128),
                  pl.BlockSpec((1, 8, chunk), lambda i, c: (i, 0, c)),
                  state(c0), state(n0), state(m0),
                  pl.BlockSpec((1, v_w), lambda i, c: (0, 0))],
        out_specs=(tok(v_w), state(c0), state(n0), state(m0)),
        out_shape=(jax.ShapeDtypeStruct((n, v_w), BF16),
                   jax.ShapeDtypeStruct(c0.shape, F32), jax.ShapeDtypeStruct(n0.shape, F32),
                   jax.ShapeDtypeStruct(m0.shape, F32)),
        scratch_shapes=[pltpu.VMEM((8, chunk), F32)],
        compiler_params=_cparams(("parallel", "arbitrary")),
    )(q, k, v, o, gates_col, gates_row, c0, n0, m0, g_head)


def _flash_body(q_ref, k_ref, v_ref, o_ref, m_sc, l_sc, acc_sc):
    tile = q_ref.shape[1]
    qi = pl.program_id(2)
    q = q_ref[0]
    m_sc[...] = jnp.full(m_sc.shape, -jnp.inf, F32)
    l_sc[...] = jnp.zeros(l_sc.shape, F32)
    acc_sc[...] = jnp.zeros(acc_sc.shape, F32)

    def step(j, masked):
        off = pl.multiple_of(j * tile, tile)
        kt = k_ref[0, pl.ds(off, tile), :]
        vt = v_ref[0, pl.ds(off, tile), :]
        s = _dot_nt(q, kt)
        if masked:
            shift = STREAM_CHUNK.bit_length() - 1
            qc = lax.broadcasted_iota(jnp.int32, s.shape, 0) >> shift
            kc = lax.broadcasted_iota(jnp.int32, s.shape, 1) >> shift
            s = jnp.where(kc <= qc, s, -jnp.inf)
        m_old = m_sc[...]
        m_new = jnp.maximum(m_old, jnp.max(s, axis=-1, keepdims=True))
        alpha = jnp.exp(m_old - m_new)
        p = jnp.exp(s - m_new)
        l_sc[...] = alpha * l_sc[...] + jnp.sum(p, axis=-1, keepdims=True)
        acc_sc[...] = alpha * acc_sc[...] + _dot(p.astype(BF16), vt)
        m_sc[...] = m_new

    def body(j, carry):
        step(j, False)
        return carry

    lax.fori_loop(0, qi, body, 0)
    step(qi, True)
    o_ref[...] = (acc_sc[...] / l_sc[...]).astype(BF16)


def _flash(qcat, kcat, vv, batch, seq, tile):
    heads, n, _ = qcat.shape
    nq = seq // tile
    return pl.pallas_call(
        _flash_body,
        grid=(batch, heads, nq),
        in_specs=[pl.BlockSpec((1, tile, QK_PAD), lambda b, h, i: (h, b * nq + i, 0)),
                  pl.BlockSpec((1, seq, QK_PAD), lambda b, h, i: (h, b, 0)),
                  pl.BlockSpec((1, seq, MLA_DV), lambda b, h, i: (h, b, 0))],
        out_specs=pl.BlockSpec((tile, MLA_DV), lambda b, h, i: (b * nq + i, h)),
        out_shape=jax.ShapeDtypeStruct((n, heads * MLA_DV), BF16),
        scratch_shapes=[pltpu.VMEM((tile, 1), F32), pltpu.VMEM((tile, 1), F32), pltpu.VMEM((tile, MLA_DV), F32)],
        compiler_params=_cparams(("parallel", "parallel", "arbitrary")),
    )(qcat, kcat, vv)


def _sattn_body(q_ref, clat_ref, nlat_ref, ckr_ref, nk_ref, wuk_ref, wuv_ref, y_ref, qa_sc, qr_sc):
    s_new = nlat_ref.shape[0]
    for hd in range(MLA_HEADS):
        qh = q_ref[hd]
        hs = slice(hd * MLA_DNOPE, (hd + 1) * MLA_DNOPE)
        qa_sc[hd * s_new:(hd + 1) * s_new, :] = _dot_nt(qh[:, 0:MLA_DNOPE], wuk_ref[:, hs]).astype(BF16)
        qr_sc[hd * s_new:(hd + 1) * s_new, :] = qh[:, MLA_DNOPE:]
    qa = qa_sc[...]
    qr = qr_sc[...]
    clat = clat_ref[0].astype(BF16)
    nlat = nlat_ref[...].astype(BF16)
    ckr = ckr_ref[0].astype(BF16)
    nkr = nk_ref[0, :, MLA_DNOPE:]
    s_c = _dot_nt(qa, clat) + _dot_nt(qr, ckr)
    s_n = _dot_nt(qa, nlat) + _dot_nt(qr, nkr)
    m = jnp.maximum(jnp.max(s_c, axis=-1, keepdims=True), jnp.max(s_n, axis=-1, keepdims=True))
    p_c = jnp.exp(s_c - m)
    p_n = jnp.exp(s_n - m)
    l = jnp.sum(p_c, axis=-1, keepdims=True) + jnp.sum(p_n, axis=-1, keepdims=True)
    o_lat = ((_dot(p_c.astype(BF16), clat) + _dot(p_n.astype(BF16), nlat)) / l).astype(BF16)
    for hd in range(MLA_HEADS):
        hs = slice(hd * MLA_DV, (hd + 1) * MLA_DV)
        y_ref[:, hs] = _dot(o_lat[hd * s_new:(hd + 1) * s_new, :], wuv_ref[:, hs]).astype(BF16)


def _sattn(qcat, cache_lat, new_lat, cache_kr_pad, kcat, w_uk, w_uv, s_new):
    heads, n, _ = qcat.shape
    b, past, _ = cache_lat.shape
    const = lambda a: pl.BlockSpec(a.shape, lambda i: (0,) * a.ndim)
    return pl.pallas_call(
        _sattn_body,
        grid=(b,),
        in_specs=[pl.BlockSpec((heads, s_new, QK_PAD), lambda i: (0, i, 0)),
                  pl.BlockSpec((1, past, MLA_KV_RANK), lambda i: (i, 0, 0)),
                  pl.BlockSpec((s_new, MLA_KV_RANK), lambda i: (i, 0)),
                  pl.BlockSpec((1, past, 128), lambda i: (i, 0, 0)),
                  pl.BlockSpec((1, s_new, QK_PAD), lambda i: (0, i, 0)),
                  const(w_uk), const(w_uv)],
        out_specs=pl.BlockSpec((s_new, heads * MLA_DV), lambda i: (i, 0)),
        out_shape=jax.ShapeDtypeStruct((n, heads * MLA_DV), BF16),
        scratch_shapes=[pltpu.VMEM((heads * s_new, MLA_KV_RANK), BF16), pltpu.VMEM((heads * s_new, 128), BF16)],
        compiler_params=_cparams(("parallel",)),
    )(qcat, cache_lat, new_lat, cache_kr_pad, kcat, w_uk, w_uv)


def _tail_body(x_ref, yml_ref, ymla_ref, mod_ref, gpre_ref, gpost_ref, gffn_ref, wg_ref, bg_ref,
               wml_ref, wmla_ref, wout_ref, wrh_ref, wrl_ref, br_ref,
               x1_ref, h2_ref, te_ref, tw_ref):
    x = x_ref[...]
    d = x.shape[1]
    sh1, sc1, gt1, sh2, sc2 = (mod_ref[j, 0] for j in range(5))
    hb = (_rms(x, gpre_ref[...]) * (1.0 + sc1) + sh1).astype(BF16)
    g = jax.nn.sigmoid(_dot(hb, wg_ref[...]) + bg_ref[...])
    yb = g[:, :d] * _dot(yml_ref[...], wml_ref[...]) + g[:, d:] * _dot(ymla_ref[...], wmla_ref[...])
    y = _dot(yb.astype(BF16), wout_ref[...])
    x1 = x + gt1 * _rms(y, gpost_ref[...])
    x1_ref[...] = x1
    h2 = _rms(x1, gffn_ref[...]) * (1.0 + sc2) + sh2
    h2_ref[...] = h2
    hi = h2.astype(BF16)
    lo = (h2 - hi.astype(F32)).astype(BF16)
    logits = _dot(hi, wrh_ref[...]) + _dot(hi, wrl_ref[...]) + _dot(lo, wrh_ref[...]) + br_ref[...]
    lane = lax.broadcasted_iota(jnp.int32, logits.shape, 1)
    lane_f = lane.astype(F32)
    lg = jnp.where(lane < N_EXPERTS, logits, -jnp.inf)
    te = jnp.zeros(lg.shape, F32)
    tw = jnp.zeros(lg.shape, F32)
    top0 = None
    for kk in range(TOP_K):
        mx = jnp.max(lg, axis=-1, keepdims=True)
        idx = jnp.min(jnp.where(lg == mx, lane_f, 128.0), axis=-1, keepdims=True)
        if kk == 0:
            top0 = mx
        te = jnp.where(lane == kk, idx, te)
        tw = jnp.where(lane == kk, jnp.exp(mx - top0), tw)
        lg = jnp.where(lane_f == idx, -jnp.inf, lg)
    te_ref[...] = te.astype(jnp.int32)
    tw_ref[...] = tw / jnp.sum(tw, axis=-1, keepdims=True)


def _tail(x, yml, ymla, mods, w, tile, tiles_per_mod):
    n, d = x.shape
    nm, _, r, _ = mods.shape
    assert nm == 5
    const = lambda a: pl.BlockSpec(a.shape, lambda i: (0,) * a.ndim)
    tok = lambda width: pl.BlockSpec((tile, width), lambda i: (i, 0))
    names = ("g_pre_mix", "g_post_mix", "g_pre_ffn", "w_gate", "b_gate", "w_br_ml", "w_br_mla", "w_out",
             "w_router_hi", "w_router_lo", "b_router")
    return pl.pallas_call(
        _tail_body,
        grid=(n // tile,),
        in_specs=[tok(d), tok(d), tok(d),
                  pl.BlockSpec((5, 1, r, d), lambda i: (0, i // tiles_per_mod, 0, 0))]
                 + [const(w[k]) for k in names],
        out_specs=(tok(d), tok(d), tok(128), tok(128)),
        out_shape=(jax.ShapeDtypeStruct((n, d), F32), jax.ShapeDtypeStruct((n, d), F32),
                   jax.ShapeDtypeStruct((n, 128), jnp.int32), jax.ShapeDtypeStruct((n, 128), F32)),
        compiler_params=_cparams(("parallel",)),
    )(x, yml, ymla, mods, *[w[k] for k in names])


def _row_copy_in(h2_hbm, xbuf, sem, src_ref, i):
    return pltpu.make_async_copy(h2_hbm.at[pl.ds(src_ref[0, 0, i], 1)], xbuf.at[pl.ds(i, 1)], sem)


def _row_copy_out(obuf, out_hbm, sem, dst_ref, i):
    return pltpu.make_async_copy(obuf.at[pl.ds(i, 1)], out_hbm.at[pl.ds(dst_ref[0, 0, i], 1)], sem)


def _moe_body(be_ref, nu_ref, src_ref, dst_ref, h2_hbm, wup_ref, bup_ref, wdn_ref, bdn_ref, out_hbm,
              xbuf, obuf, sem_in, sem_out):
    del be_ref
    rows = xbuf.shape[0]
    ff = wdn_ref.shape[1]

    @pl.when(pl.program_id(0) < nu_ref[0])
    def _():
        def issue_in(i, c):
            _row_copy_in(h2_hbm, xbuf, sem_in, src_ref, i).start()
            return c

        def wait_in(i, c):
            _row_copy_in(h2_hbm, xbuf, sem_in, src_ref, i).wait()
            return c

        lax.fori_loop(0, rows, issue_in, 0)
        lax.fori_loop(0, rows, wait_in, 0)

        gu = _dot(xbuf[...].astype(BF16), wup_ref[0]) + bup_ref[0]
        gl = jnp.minimum(gu[:, :ff], SWIGLU_LIMIT)
        up = jnp.clip(gu[:, ff:], -SWIGLU_LIMIT, SWIGLU_LIMIT)
        act = (up + 1.0) * gl * jax.nn.sigmoid(SWIGLU_ALPHA * gl)
        obuf[...] = _dot(act.astype(BF16), wdn_ref[0]) + bdn_ref[0]

        def issue_out(i, c):
            _row_copy_out(obuf, out_hbm, sem_out, dst_ref, i).start()
            return c

        def wait_out(i, c):
            _row_copy_out(obuf, out_hbm, sem_out, dst_ref, i).wait()
            return c

        lax.fori_loop(0, rows, issue_out, 0)
        lax.fori_loop(0, rows, wait_out, 0)


def _moe(block_e, n_used, src, dst, h2, w_up, b_up, w_down, b_down, out_rows):
    nb = src.shape[0]
    d = h2.shape[1]
    ff2 = w_up.shape[2]
    ff = w_down.shape[1]
    idx_spec = pl.BlockSpec((1, 1, MOE_ROWS), lambda j, be, nu: (j, 0, 0), memory_space=pltpu.SMEM)
    grid_spec = pltpu.PrefetchScalarGridSpec(
        num_scalar_prefetch=2,
        grid=(nb,),
        in_specs=[idx_spec, idx_spec,
                  pl.BlockSpec(memory_space=pl.ANY),
                  pl.BlockSpec((1, d, ff2), lambda j, be, nu: (be[j], 0, 0)),
                  pl.BlockSpec((1, 1, ff2), lambda j, be, nu: (be[j], 0, 0)),
                  pl.BlockSpec((1, ff, d), lambda j, be, nu: (be[j], 0, 0)),
                  pl.BlockSpec((1, 1, d), lambda j, be, nu: (be[j], 0, 0))],
        out_specs=pl.BlockSpec(memory_space=pl.ANY),
        scratch_shapes=[pltpu.VMEM((MOE_ROWS, d), F32), pltpu.VMEM((MOE_ROWS, d), F32),
                        pltpu.SemaphoreType.DMA, pltpu.SemaphoreType.DMA],
    )
    return pl.pallas_call(
        _moe_body,
        grid_spec=grid_spec,
        out_shape=jax.ShapeDtypeStruct((out_rows, d), F32),
        compiler_params=_cparams(("arbitrary",)),
    )(block_e, n_used, src, dst, h2, w_up, b_up, w_down, b_down)


def _combine_body(o0_ref, o1_ref, o2_ref, o3_ref, tw_ref, x1_ref, mod_ref, g_ref, y_ref):
    tw = tw_ref[...]
    lane = lax.broadcasted_iota(jnp.int32, tw.shape, 1)
    acc = None
    for kk, o_ref in enumerate((o0_ref, o1_ref, o2_ref, o3_ref)):
        wk = jnp.sum(jnp.where(lane == kk, tw, 0.0), axis=-1, keepdims=True)
        term = o_ref[...] * wk
        acc = term if acc is None else acc + term
    y_ref[...] = x1_ref[...] + mod_ref[0, 0] * _rms(acc, g_ref[...])


def _combine(moe_out, tw, x1, mods, g_post_ffn, n_pad, row0, rows, tile, tiles_per_mod):
    d = x1.shape[1]
    r = mods.shape[2]
    assert row0 % tile == 0 and n_pad % tile == 0 and rows % tile == 0
    t0 = row0 // tile
    kt = n_pad // tile
    ospec = lambda kk: pl.BlockSpec((tile, d), lambda i: (kk * kt + t0 + i, 0))
    return pl.pallas_call(
        _combine_body,
        grid=(rows // tile,),
        in_specs=[ospec(0), ospec(1), ospec(2), ospec(3),
                  pl.BlockSpec((tile, 128), lambda i: (t0 + i, 0)),
                  pl.BlockSpec((tile, d), lambda i: (t0 + i, 0)),
                  pl.BlockSpec((1, 1, r, d), lambda i: (0, i // tiles_per_mod, 0, 0)),
                  pl.BlockSpec((1, d), lambda i: (0, 0))],
        out_specs=pl.BlockSpec((tile, d), lambda i: (i, 0)),
        out_shape=jax.ShapeDtypeStruct((rows, d), F32),
        compiler_params=_cparams(("parallel",)),
    )(moe_out, moe_out, moe_out, moe_out, tw, x1, mods, g_post_ffn)


def _rope_table(pos):
    inv = 1.0 / (ROPE_THETA ** (jnp.arange(0, MLA_DROPE, 2, dtype=F32) / MLA_DROPE))
    ang = pos.astype(F32)[:, None] * inv[None, :]
    c, s = jnp.cos(ang), jnp.sin(ang)
    return jnp.concatenate([c, c, s, s], axis=-1)


def _rotate_cols(w):
    half = w.shape[-1] // 2
    return jnp.concatenate([-w[..., half:], w[..., :half]], axis=-1)


def _prep_weights(w_in, b_in, ml_f_bias, mla_g_q, w_uq, mla_g_kv, w_uk, w_uv, g_pre_mix):
    qk_w = ML_HEADS * ML_DQK
    v_w = ML_HEADS * ML_DV
    sizes = [qk_w, qk_w, v_w, ML_HEADS, ML_HEADS, v_w, MLA_Q_RANK, MLA_KV_RANK, MLA_DROPE]
    pts = [int(p) for p in np.cumsum(sizes)[:-1]]
    wq, wk, wv, wi, wf, wo, wql, wkv, wpe = jnp.split(w_in, pts, axis=-1)
    bq, bk, bv, bi, bf, bo, bql, bkv, bpe = jnp.split(b_in, pts, axis=-1)
    d = w_in.shape[0]
    pad_w = jnp.zeros((d, 128 - 2 * ML_HEADS), F32)
    pad_b = jnp.zeros((128 - 2 * ML_HEADS,), F32)
    w_small = jnp.concatenate([wql, wkv, wpe, _rotate_cols(wpe), wi, wf, pad_w], axis=-1)
    b_small = jnp.concatenate([bql, bkv, bpe, _rotate_cols(bpe), bi, bf, pad_b], axis=-1)
    f_bias = jnp.concatenate([jnp.zeros((ML_HEADS,), F32), ml_f_bias, pad_b], axis=-1)
    wuq_h = w_uq.reshape(MLA_Q_RANK, MLA_HEADS, MLA_DNOPE + MLA_DROPE)
    wuq_r = wuq_h[..., MLA_DNOPE:]
    wuq_ext = jnp.concatenate([wuq_h, _rotate_cols(wuq_r)], axis=-1).reshape(MLA_Q_RANK, MLA_HEADS * QK_PAD)
    return {
        "g_pre_mix": g_pre_mix[None, :],
        "w_main": jnp.concatenate([wq, wk, wv, wo], axis=-1).astype(BF16),
        "b_main": jnp.concatenate([bq, bk, bv, bo], axis=-1)[None, :],
        "w_small": w_small.astype(BF16), "b_small": b_small[None, :], "f_bias": f_bias[None, :],
        "g_q": mla_g_q[None, :], "g_kv": mla_g_kv[None, :],
        "w_uq": wuq_ext.astype(BF16), "w_uk": w_uk.astype(BF16), "w_uv": w_uv.astype(BF16),
    }


def _route_blocks(te, n_tok, n_pad):
    n_assign = n_tok * TOP_K
    nb = -(-n_assign // MOE_ROWS) + N_EXPERTS
    ids = jnp.arange(n_assign, dtype=jnp.int32)
    skey = jnp.sort(te.reshape(-1) * (1 << ORDER_BITS) + ids)
    order = skey & ((1 << ORDER_BITS) - 1)
    bounds = jnp.searchsorted(skey, jnp.arange(N_EXPERTS + 1, dtype=jnp.int32) * (1 << ORDER_BITS)).astype(jnp.int32)
    start, counts = bounds[:-1], bounds[1:] - bounds[:-1]
    nblk = (counts + MOE_ROWS - 1) // MOE_ROWS
    blk_end = jnp.cumsum(nblk)
    n_used = blk_end[-1]
    j = jnp.arange(nb, dtype=jnp.int32)
    jc = jnp.minimum(j, n_used - 1)
    be = jnp.minimum(jnp.searchsorted(blk_end, jc, side="right"), N_EXPERTS - 1).astype(jnp.int32)
    jj = jc - (blk_end - nblk)[be]
    p0 = start[be] + jj * MOE_ROWS
    nv = jnp.where(j < n_used, jnp.clip(counts[be] - jj * MOE_ROWS, 0, MOE_ROWS), 0)
    order_p = jnp.concatenate([order, jnp.zeros((MOE_ROWS,), jnp.int32)])
    idx = jax.vmap(lambda p: lax.dynamic_slice(order_p, (p,), (MOE_ROWS,)))(p0)
    r = jnp.arange(MOE_ROWS, dtype=jnp.int32)[None, :]
    src = idx >> 2
    dst = jnp.where(r < nv[:, None], (idx & 3) * n_pad + src, TOP_K * n_pad + r)
    return be, n_used.reshape(1).astype(jnp.int32), src[:, None, :], dst[:, None, :]


def kernel(x_prompt, x_sample, cache_kv_latent, cache_k_rope, state_mlstm_C, state_mlstm_n, state_mlstm_m,
           c_prompt, c_sample, w_ada, b_ada, g_pre_mix, g_post_mix, g_pre_ffn, g_post_ffn, w_in, b_in,
           ml_f_bias, ml_head_g, mla_g_q, w_uq, mla_g_kv, w_uk, w_uv, w_br_ml, w_br_mla, w_gate, b_gate,
           w_out, w_router, b_router, w_up, b_up, w_down, b_down):
    assert w_ada.shape[0] == 1, "single-layer step"
    bp, sp, d = x_prompt.shape
    bs, ss, _ = x_sample.shape
    past = cache_kv_latent.shape[2]
    n_p, n_s = bp * sp, bs * ss
    n_tok = n_p + n_s
    tile_p = TOK_TILE
    tile_s = min(TOK_TILE, n_s)
    n_pad = -(-n_tok // tile_p) * tile_p
    assert n_tok * TOP_K < (1 << ORDER_BITS)
    assert sp % ATT_TILE == 0 and sp % ML_CHUNK == 0 and sp % tile_p == 0
    assert n_s % tile_s == 0 and n_p % tile_s == 0 and n_pad % tile_s == 0

    mods = _ada(jnp.concatenate([c_prompt, c_sample], axis=0), w_ada[0], b_ada[0][None, :])
    mods = mods.reshape(bp + bs, 6, d).transpose(1, 0, 2)
    mods_p = mods[:, :bp][:, :, None, :]
    mods_s = jnp.repeat(mods[:, bp:], ss, axis=1).reshape(6, n_s // tile_s, tile_s, d)

    wm = _prep_weights(w_in[0], b_in[0], ml_f_bias[0], mla_g_q[0], w_uq[0], mla_g_kv[0], w_uk[0], w_uv[0],
                       g_pre_mix[0])
    cs_p = _rope_table(jnp.arange(sp))
    cs_s = jnp.tile(_rope_table(past + jnp.arange(ss)), (bs, 1))

    xp = x_prompt.reshape(n_p, d)
    xs = x_sample.reshape(n_s, d)
    tiles_per_stream = sp // tile_p
    mp = _mixer_in(xp, mods_p[:2], cs_p, wm, tile_p, tiles_per_stream, tiles_per_stream)
    ms = _mixer_in(xs, mods_s[:2], cs_s, wm, tile_s, 1, n_s // tile_s)

    g_head = ml_head_g[0][None, :]

    def gate_rows(g, b, s):
        return g[:, :2 * ML_HEADS].reshape(b, s, 2 * ML_HEADS).transpose(0, 2, 1)

    qp, kp, vp, op, gp, lat_p, kr_p, qcat_p, kcat_p, vv_p = mp
    zc = jnp.zeros((bp, ML_HEADS, ML_DQK, ML_DV), F32)
    zn = jnp.zeros((bp, ML_HEADS, 1, ML_DQK), F32)
    zm = jnp.zeros((bp, ML_HEADS, 1, 1), F32)
    yml_p, c_p, nn_p, m_p = _mlstm(qp, kp, vp, op, gp, gate_rows(gp, bp, sp), zc, zn, zm, g_head, ML_CHUNK)
    ymla_p = _flash(qcat_p, kcat_p, vv_p, bp, sp, ATT_TILE)

    qs, ks, vs, os_, gs, lat_s, kr_s, qcat_s, kcat_s, _ = ms
    yml_s, c_s, nn_s, m_s = _mlstm(qs, ks, vs, os_, gs, gate_rows(gs, bs, ss),
                                   state_mlstm_C[0], state_mlstm_n[0][:, :, None, :],
                                   state_mlstm_m[0][:, :, None, None], g_head, ss)
    ckr_pad = jnp.pad(cache_k_rope[0], ((0, 0), (0, 0), (0, 128 - MLA_DROPE)))
    ymla_s = _sattn(qcat_s, cache_kv_latent[0], lat_s, ckr_pad, kcat_s, wm["w_uk"], wm["w_uv"], ss)

    w_r = jnp.pad(w_router[0], ((0, 0), (0, 128 - N_EXPERTS)))
    w_r_hi = w_r.astype(BF16)
    wt = {
        "g_pre_mix": g_pre_mix[0][None, :], "g_post_mix": g_post_mix[0][None, :], "g_pre_ffn": g_pre_ffn[0][None, :],
        "w_gate": w_gate[0].astype(BF16), "b_gate": b_gate[0][None, :],
        "w_br_ml": w_br_ml[0].astype(BF16), "w_br_mla": w_br_mla[0].astype(BF16), "w_out": w_out[0].astype(BF16),
        "w_router_hi": w_r_hi, "w_router_lo": (w_r - w_r_hi.astype(F32)).astype(BF16),
        "b_router": jnp.pad(b_router[0], (0, 128 - N_EXPERTS))[None, :],
    }
    x1_p, h2_p, te_p, tw_p = _tail(xp, yml_p, ymla_p, mods_p[:5], wt, tile_p, tiles_per_stream)
    x1_s, h2_s, te_s, tw_s = _tail(xs, yml_s, ymla_s, mods_s[:5], wt, tile_s, 1)

    x1 = jnp.concatenate([x1_p, x1_s], axis=0)
    h2 = jnp.concatenate([h2_p, h2_s], axis=0)
    te = jnp.concatenate([te_p, te_s], axis=0)[:, :TOP_K]
    tw = jnp.concatenate([tw_p, tw_s], axis=0)

    be, n_used, src, dst = _route_blocks(te, n_tok, n_pad)
    moe_out = _moe(be, n_used, src, dst, h2, w_up[0].astype(BF16), b_up[0][:, None, :],
                   w_down[0].astype(BF16), b_down[0][:, None, :], n_pad * TOP_K + MOE_ROWS)
    gff = g_post_ffn[0][None, :]
    y_p = _combine(moe_out, tw, x1, mods_p[5:], gff, n_pad, 0, n_p, tile_p, tiles_per_stream)
    y_s = _combine(moe_out, tw, x1, mods_s[5:], gff, n_pad, n_p, n_s, tile_s, 1)

    return (y_p.reshape(bp, sp, d), y_s.reshape(bs, ss, d),
            lat_p.reshape(1, bp, sp, MLA_KV_RANK), kr_p.reshape(1, bp, sp, MLA_DROPE),
            c_p[None], nn_p.reshape(1, bp, ML_HEADS, ML_DQK), m_p.reshape(1, bp, ML_HEADS),
            lat_s.reshape(1, bs, ss, MLA_KV_RANK), kr_s.reshape(1, bs, ss, MLA_DROPE),
            c_s[None], nn_s.reshape(1, bs, ML_HEADS, ML_DQK), m_s.reshape(1, bs, ML_HEADS))
```

```python
import functools

import jax
import jax.numpy as jnp
import numpy as np
from jax import lax
from jax.experimental import pallas as pl
from jax.experimental.pallas import tpu as pltpu

F32 = jnp.float32
BF16 = jnp.bfloat16

EPS = 1e-6
ROPE_THETA = 10000.0
SWIGLU_ALPHA = 1.702
SWIGLU_LIMIT = 7.0
TOP_K = 4

ML_HEADS = 4
ML_DQK = 128
ML_DV = 256
MLA_HEADS = 8
MLA_DNOPE = 128
MLA_DROPE = 64
MLA_DV = 128
MLA_Q_RANK = 384
MLA_KV_RANK = 256
MLA_SCALE = (MLA_DNOPE + MLA_DROPE) ** -0.5
Q_SCALE = MLA_SCALE * float(np.log2(np.e))
QK_PAD = 256
N_EXPERTS = 32

VMEM_LIMIT_BYTES = 52 * 1024 * 1024

TOK_TILE = 256
ML_CHUNK = 256
ATT_TILE = 512
FLASH_SPLIT = 2
STREAM_CHUNK = 64
MOE_ROWS = 512
ORDER_BITS = 18


def _cparams(sem, vmem=VMEM_LIMIT_BYTES):
    return pltpu.CompilerParams(dimension_semantics=sem, vmem_limit_bytes=vmem)


def _dot(a, b):
    return jnp.dot(a, b, preferred_element_type=F32)


def _dot_nt(a, b):
    return lax.dot_general(a, b, (((1,), (1,)), ((), ())), preferred_element_type=F32)


def _rms(x, g):
    return x * lax.rsqrt(jnp.mean(x * x, axis=-1, keepdims=True) + EPS) * g


def _split3(x):
    hi = x.astype(BF16)
    r1 = x - hi.astype(F32)
    mid = r1.astype(BF16)
    lo = (r1 - mid.astype(F32)).astype(BF16)
    return hi, mid, lo


def _ada_body(c_ref, w_ref, b_ref, o_ref):
    o_ref[...] = _dot(c_ref[...].astype(BF16), w_ref[...].astype(BF16)) + b_ref[...]


def _ada(c, w, b):
    m, d = c.shape
    n = w.shape[1]
    return pl.pallas_call(
        _ada_body,
        grid=(n // d,),
        in_specs=[pl.BlockSpec((m, d), lambda j: (0, 0)),
                  pl.BlockSpec((d, d), lambda j: (0, j)),
                  pl.BlockSpec((1, d), lambda j: (0, j))],
        out_specs=pl.BlockSpec((m, d), lambda j: (0, j)),
        out_shape=jax.ShapeDtypeStruct((m, n), F32),
        compiler_params=_cparams(("parallel",)), name="ada",
    )(c, w, b)


_SM_QLAT = 0
_SM_KVLAT = MLA_Q_RANK
_SM_KPE = MLA_Q_RANK + MLA_KV_RANK
_SM_GATE = _SM_KPE + 128
_SM_W = _SM_GATE + 128


def _mixer_in_body(x_ref, mod_ref, gpre_ref, wm_ref, bm_ref, ws_ref, bs_ref, fb_ref, gq_ref, gkv_ref,
                   wuq_ref, wuk_ref, wuvt_ref, cs_ref,
                   q_ref, k_ref, v_ref, o_ref, gate_ref, lat_ref, kr_ref, qcat_ref, kcat_ref, vvt_ref):
    x = x_ref[...]
    shift = mod_ref[0, 0]
    scale = mod_ref[1, 0]
    h = _rms(x, gpre_ref[...]) * (1.0 + scale) + shift
    hb = h.astype(BF16)
    qk_w = ML_HEADS * ML_DQK
    v_w = ML_HEADS * ML_DV
    q_ref[...] = (_dot(hb, wm_ref[:, 0:qk_w]) + bm_ref[:, 0:qk_w]).astype(BF16)
    k_ref[...] = ((_dot(hb, wm_ref[:, qk_w:2 * qk_w]) + bm_ref[:, qk_w:2 * qk_w]) * (ML_DQK ** -0.5)).astype(BF16)
    v_ref[...] = (_dot(hb, wm_ref[:, 2 * qk_w:2 * qk_w + v_w]) + bm_ref[:, 2 * qk_w:2 * qk_w + v_w]).astype(BF16)
    o_ref[...] = (_dot(hb, wm_ref[:, 2 * qk_w + v_w:]) + bm_ref[:, 2 * qk_w + v_w:]).astype(BF16)

    zs = _dot(hb, ws_ref[...]) + bs_ref[...]
    g = zs[:, _SM_GATE:_SM_GATE + 128]
    gf = g + fb_ref[...]
    logsig = jnp.minimum(gf, 0.0) - jnp.log1p(jnp.exp(-jnp.abs(gf)))
    lane = lax.broadcasted_iota(jnp.int32, g.shape, 1)
    gate_ref[...] = jnp.where(lane >= ML_HEADS, logsig, g)

    cs = cs_ref[...]
    qln = _rms(zs[:, _SM_QLAT:_SM_QLAT + MLA_Q_RANK], gq_ref[...]).astype(BF16)
    cq = _dot(qln, wuq_ref[...])
    for hd in range(MLA_HEADS):
        blk = cq[:, hd * QK_PAD:(hd + 1) * QK_PAD]
        t = blk[:, MLA_DNOPE:] * cs
        rot = t + pltpu.roll(t, 64, 1)
        qcat_ref[hd, :, 0:MLA_DNOPE] = (blk[:, :MLA_DNOPE] * Q_SCALE).astype(BF16)
        qcat_ref[hd, :, MLA_DNOPE:] = (rot * Q_SCALE).astype(BF16)
    latent = _rms(zs[:, _SM_KVLAT:_SM_KVLAT + MLA_KV_RANK], gkv_ref[...])
    lat_ref[...] = latent
    tk = zs[:, _SM_KPE:_SM_KPE + 128] * cs
    rotk = tk + pltpu.roll(tk, 64, 1)
    kr_ref[...] = rotk[:, 0:MLA_DROPE]
    krz = jnp.where(lane < MLA_DROPE, rotk, 0.0).astype(BF16)
    latb = latent.astype(BF16)
    kn = _dot(latb, wuk_ref[...])
    for hd in range(MLA_HEADS):
        kcat_ref[hd, :, 0:MLA_DNOPE] = kn[:, hd * MLA_DNOPE:(hd + 1) * MLA_DNOPE].astype(BF16)
        kcat_ref[hd, :, MLA_DNOPE:] = krz
        vvt_ref[hd] = _dot_nt(wuvt_ref[hd], latb).astype(BF16)


def _mixer_in(x, mods, cs, w, tile, tiles_per_mod, cs_tiles):
    n, d = x.shape
    nm, _, r, _ = mods.shape
    grid = (n // tile,)
    const = lambda a: pl.BlockSpec(a.shape, lambda i: (0,) * a.ndim)
    tok = lambda width: pl.BlockSpec((tile, width), lambda i: (i, 0))
    head = lambda width: pl.BlockSpec((MLA_HEADS, tile, width), lambda i: (0, i, 0))
    qk_w = ML_HEADS * ML_DQK
    v_w = ML_HEADS * ML_DV
    out_shape = (
        jax.ShapeDtypeStruct((n, qk_w), BF16), jax.ShapeDtypeStruct((n, qk_w), BF16),
        jax.ShapeDtypeStruct((n, v_w), BF16), jax.ShapeDtypeStruct((n, v_w), BF16),
        jax.ShapeDtypeStruct((n, 128), F32),
        jax.ShapeDtypeStruct((n, MLA_KV_RANK), F32), jax.ShapeDtypeStruct((n, MLA_DROPE), F32),
        jax.ShapeDtypeStruct((MLA_HEADS, n, QK_PAD), BF16), jax.ShapeDtypeStruct((MLA_HEADS, n, QK_PAD), BF16),
        jax.ShapeDtypeStruct((MLA_HEADS, MLA_DV, n), BF16),
    )
    out_specs = (tok(qk_w), tok(qk_w), tok(v_w), tok(v_w), tok(128), tok(MLA_KV_RANK), tok(MLA_DROPE),
                 head(QK_PAD), head(QK_PAD), pl.BlockSpec((MLA_HEADS, MLA_DV, tile), lambda i: (0, 0, i)))
    in_specs = [
        tok(d),
        pl.BlockSpec((nm, 1, r, d), lambda i: (0, i // tiles_per_mod, 0, 0)),
        const(w["g_pre_mix"]), const(w["w_main"]), const(w["b_main"]), const(w["w_small"]), const(w["b_small"]),
        const(w["f_bias"]), const(w["g_q"]), const(w["g_kv"]), const(w["w_uq"]), const(w["w_uk"]), const(w["w_uv_t"]),
        pl.BlockSpec((tile, 128), lambda i: (i % cs_tiles, 0)),
    ]
    return pl.pallas_call(
        _mixer_in_body, grid=grid, in_specs=in_specs, out_specs=out_specs, out_shape=out_shape,
        compiler_params=_cparams(("parallel",)), name="mixer_in",
    )(x, mods, w["g_pre_mix"], w["w_main"], w["b_main"], w["w_small"], w["b_small"], w["f_bias"],
      w["g_q"], w["g_kv"], w["w_uq"], w["w_uk"], w["w_uv_t"], cs)


def _mlstm_body(q_ref, k_ref, v_ref, o_ref, gc_ref, gr_ref, c0_ref, n0_ref, m0_ref, gh_ref,
                y_ref, c_ref, n_ref, m_ref, br_ref):
    chunk = q_ref.shape[0]

    @pl.when(pl.program_id(1) == 0)
    def _():
        c_ref[...] = c0_ref[...]
        n_ref[...] = n0_ref[...]
        m_ref[...] = m0_ref[...]

    row = lax.broadcasted_iota(jnp.int32, (chunk, chunk), 0)
    col = lax.broadcasted_iota(jnp.int32, (chunk, chunk), 1)
    causal = col <= row
    tri = causal.astype(BF16)
    tri_t = (row <= col).astype(BF16)

    gc = gc_ref[...]
    gr = gr_ref[0]
    bc_all = sum(_dot(tri, p) for p in _split3(gc))
    br_ref[...] = sum(_dot(p, tri_t) for p in _split3(gr))

    lane = lax.broadcasted_iota(jnp.int32, gc.shape, 1)
    rowv = lax.broadcasted_iota(jnp.int32, (chunk, 1), 0)
    is_last = rowv == chunk - 1

    def pick(a, j):
        return jnp.sum(jnp.where(lane == j, a, 0.0), axis=-1, keepdims=True)

    def last(a):
        return jnp.sum(jnp.where(is_last, a, 0.0), axis=0, keepdims=True)

    for hd in range(ML_HEADS):
        qs = slice(hd * ML_DQK, (hd + 1) * ML_DQK)
        vs = slice(hd * ML_DV, (hd + 1) * ML_DV)
        q = q_ref[:, qs]
        k = k_ref[:, qs]
        v = v_ref[:, vs]
        ig_c = pick(gc, hd)
        b_c = pick(bc_all, ML_HEADS + hd)
        ig_r = gr_ref[0, hd:hd + 1, :]
        b_r = br_ref[ML_HEADS + hd:ML_HEADS + hd + 1, :]
        m_prev = m_ref[0, hd]
        c_prev = c_ref[0, hd]
        n_prev = n_ref[0, hd]

        dmat = jnp.where(causal, b_c - b_r + ig_r, -jnp.inf)
        inter = b_c + m_prev
        m_t = jnp.maximum(inter, jnp.max(dmat, axis=-1, keepdims=True))
        w_intra = jnp.exp(dmat - m_t)
        w_inter = jnp.exp(inter - m_t)
        qk = _dot_nt(q, k) * w_intra
        num = w_inter * _dot(q, c_prev.astype(BF16)) + _dot(qk.astype(BF16), v)
        qn = jnp.sum(q.astype(F32) * n_prev, axis=-1, keepdims=True)
        den = w_inter * qn + jnp.sum(qk, axis=-1, keepdims=True)
        hcur = num / jnp.maximum(jnp.abs(den), jnp.exp(-m_t))
        hn = hcur * lax.rsqrt(jnp.mean(hcur * hcur, axis=-1, keepdims=True) + EPS)
        y_ref[:, vs] = (hn * gh_ref[:, vs] * jax.nn.sigmoid(o_ref[:, vs].astype(F32))).astype(BF16)

        m_new = last(m_t)
        w_prev = jnp.exp(last(inter) - m_new)
        w_end = jnp.exp(last(b_c) - b_c + ig_c - m_new)
        kw = k.astype(F32) * w_end
        c_ref[0, hd] = w_prev * c_prev + _dot(kw.T.astype(BF16), v)
        n_ref[0, hd] = w_prev * n_prev + jnp.sum(kw, axis=0, keepdims=True)
        m_ref[0, hd] = m_new


def _mlstm(q, k, v, o, gates_col, gates_row, c0, n0, m0, g_head, chunk):
    n = q.shape[0]
    b, _, s = gates_row.shape
    nc = s // chunk
    tok = lambda width: pl.BlockSpec((chunk, width), lambda i, c: (i * nc + c, 0))
    state = lambda a: pl.BlockSpec((1,) + a.shape[1:], lambda i, c: (i,) + (0,) * (a.ndim - 1))
    qk_w = ML_HEADS * ML_DQK
    v_w = ML_HEADS * ML_DV
    return pl.pallas_call(
        _mlstm_body,
        grid=(b, nc),
        in_specs=[tok(qk_w), tok(qk_w), tok(v_w), tok(v_w), tok(128),
                  pl.BlockSpec((1, 8, chunk), lambda i, c: (i, 0, c)),
                  state(c0), state(n0), state(m0),
                  pl.BlockSpec((1, v_w), lambda i, c: (0, 0))],
        out_specs=(tok(v_w), state(c0), state(n0), state(m0)),
        out_shape=(jax.ShapeDtypeStruct((n, v_w), BF16),
                   jax.ShapeDtypeStruct(c0.shape, F32), jax.ShapeDtypeStruct(n0.shape, F32),
                   jax.ShapeDtypeStruct(m0.shape, F32)),
        scratch_shapes=[pltpu.VMEM((8, chunk), F32)],
        compiler_params=_cparams(("parallel", "arbitrary")), name="mlstm",
    )(q, k, v, o, gates_col, gates_row, c0, n0, m0, g_head)


def _flash_body(q_ref, k_ref, vt_ref, o_ref, m_sc, l_sc, acc_sc):
    tile = q_ref.shape[1]
    half = tile // FLASH_SPLIT
    qi = pl.program_id(2)
    m_sc[...] = jnp.full(m_sc.shape, -jnp.inf, F32)
    l_sc[...] = jnp.zeros(l_sc.shape, F32)
    acc_sc[...] = jnp.zeros(acc_sc.shape, F32)

    def step(j, masked):
        off = pl.multiple_of(j * tile, tile)
        kt = k_ref[0, pl.ds(off, tile), :]
        vt = vt_ref[0, :, pl.ds(off, tile)]
        for part in range(FLASH_SPLIT):
            qs = slice(part * half, (part + 1) * half)
            st = _dot_nt(kt, q_ref[0, qs, :])
            if masked:
                shift = STREAM_CHUNK.bit_length() - 1
                kc = lax.broadcasted_iota(jnp.int32, st.shape, 0) >> shift
                qc = (lax.broadcasted_iota(jnp.int32, st.shape, 1) + part * half) >> shift
                st = jnp.where(kc <= qc, st, -jnp.inf)
            m_old = m_sc[:, qs]
            m_new = jnp.maximum(m_old, jnp.max(st, axis=0, keepdims=True))
            alpha = jnp.exp2(m_old - m_new)
            p = jnp.exp2(st - m_new)
            l_sc[:, qs] = alpha * l_sc[:, qs] + jnp.sum(p, axis=0, keepdims=True)
            acc_sc[:, qs] = alpha * acc_sc[:, qs] + _dot(vt, p.astype(BF16))
            m_sc[:, qs] = m_new

    def body(j, carry):
        step(j, False)
        return carry

    lax.fori_loop(0, qi, body, 0)
    step(qi, True)
    o_ref[...] = (acc_sc[...] / l_sc[...]).T.astype(BF16)


def _flash(qcat, kcat, vvt, batch, seq, tile):
    heads, n, _ = qcat.shape
    nq = seq // tile
    return pl.pallas_call(
        _flash_body,
        grid=(batch, heads, nq),
        in_specs=[pl.BlockSpec((1, tile, QK_PAD), lambda b, h, i: (h, b * nq + i, 0)),
                  pl.BlockSpec((1, seq, QK_PAD), lambda b, h, i: (h, b, 0)),
                  pl.BlockSpec((1, MLA_DV, seq), lambda b, h, i: (h, 0, b))],
        out_specs=pl.BlockSpec((tile, MLA_DV), lambda b, h, i: (b * nq + i, h)),
        out_shape=jax.ShapeDtypeStruct((n, heads * MLA_DV), BF16),
        scratch_shapes=[pltpu.VMEM((1, tile), F32), pltpu.VMEM((1, tile), F32), pltpu.VMEM((MLA_DV, tile), F32)],
        compiler_params=_cparams(("parallel", "parallel", "arbitrary")),
        name="flash",
    )(qcat, kcat, vvt)


def _sattn_body(q_ref, clat_ref, nlat_ref, ckr_ref, nk_ref, wuk_ref, wuv_ref, y_ref, qa_sc, qr_sc):
    s_new = nlat_ref.shape[0]
    for hd in range(MLA_HEADS):
        qh = q_ref[hd]
        hs = slice(hd * MLA_DNOPE, (hd + 1) * MLA_DNOPE)
        qa_sc[hd * s_new:(hd + 1) * s_new, :] = _dot_nt(qh[:, 0:MLA_DNOPE], wuk_ref[:, hs]).astype(BF16)
        qr_sc[hd * s_new:(hd + 1) * s_new, :] = qh[:, MLA_DNOPE:]
    qa = qa_sc[...]
    qr = qr_sc[...]
    clat = clat_ref[0].astype(BF16)
    nlat = nlat_ref[...].astype(BF16)
    ckr = ckr_ref[0].astype(BF16)
    nkr = nk_ref[0, :, MLA_DNOPE:]
    s_c = _dot_nt(qa, clat) + _dot_nt(qr, ckr)
    s_n = _dot_nt(qa, nlat) + _dot_nt(qr, nkr)
    m = jnp.maximum(jnp.max(s_c, axis=-1, keepdims=True), jnp.max(s_n, axis=-1, keepdims=True))
    p_c = jnp.exp2(s_c - m)
    p_n = jnp.exp2(s_n - m)
    l = jnp.sum(p_c, axis=-1, keepdims=True) + jnp.sum(p_n, axis=-1, keepdims=True)
    o_lat = ((_dot(p_c.astype(BF16), clat) + _dot(p_n.astype(BF16), nlat)) / l).astype(BF16)
    for hd in range(MLA_HEADS):
        hs = slice(hd * MLA_DV, (hd + 1) * MLA_DV)
        y_ref[:, hs] = _dot(o_lat[hd * s_new:(hd + 1) * s_new, :], wuv_ref[:, hs]).astype(BF16)


def _sattn(qcat, cache_lat, new_lat, cache_kr_pad, kcat, w_uk, w_uv, s_new):
    heads, n, _ = qcat.shape
    b, past, _ = cache_lat.shape
    const = lambda a: pl.BlockSpec(a.shape, lambda i: (0,) * a.ndim)
    return pl.pallas_call(
        _sattn_body,
        grid=(b,),
        in_specs=[pl.BlockSpec((heads, s_new, QK_PAD), lambda i: (0, i, 0)),
                  pl.BlockSpec((1, past, MLA_KV_RANK), lambda i: (i, 0, 0)),
                  pl.BlockSpec((s_new, MLA_KV_RANK), lambda i: (i, 0)),
                  pl.BlockSpec((1, past, 128), lambda i: (i, 0, 0)),
                  pl.BlockSpec((1, s_new, QK_PAD), lambda i: (0, i, 0)),
                  const(w_uk), const(w_uv)],
        out_specs=pl.BlockSpec((s_new, heads * MLA_DV), lambda i: (i, 0)),
        out_shape=jax.ShapeDtypeStruct((n, heads * MLA_DV), BF16),
        scratch_shapes=[pltpu.VMEM((heads * s_new, MLA_KV_RANK), BF16), pltpu.VMEM((heads * s_new, 128), BF16)],
        compiler_params=_cparams(("parallel",)), name="sattn",
    )(qcat, cache_lat, new_lat, cache_kr_pad, kcat, w_uk, w_uv)


def _tail_body(x_ref, yml_ref, ymla_ref, mod_ref, gpre_ref, gpost_ref, gffn_ref, wg_ref, bg_ref,
               wml_ref, wmla_ref, wout_ref, wrh_ref, wrl_ref, br_ref,
               x1_ref, h2_ref, te_ref, tw_ref):
    x = x_ref[...]
    d = x.shape[1]
    sh1, sc1, gt1, sh2, sc2 = (mod_ref[j, 0] for j in range(5))
    hb = (_rms(x, gpre_ref[...]) * (1.0 + sc1) + sh1).astype(BF16)
    g = jax.nn.sigmoid(_dot(hb, wg_ref[...]) + bg_ref[...])
    yb = g[:, :d] * _dot(yml_ref[...], wml_ref[...]) + g[:, d:] * _dot(ymla_ref[...], wmla_ref[...])
    y = _dot(yb.astype(BF16), wout_ref[...])
    x1 = x + gt1 * _rms(y, gpost_ref[...])
    x1_ref[...] = x1
    h2 = _rms(x1, gffn_ref[...]) * (1.0 + sc2) + sh2
    h2_ref[...] = h2
    hi = h2.astype(BF16)
    lo = (h2 - hi.astype(F32)).astype(BF16)
    logits = _dot(hi, wrh_ref[...]) + _dot(hi, wrl_ref[...]) + _dot(lo, wrh_ref[...]) + br_ref[...]
    lane = lax.broadcasted_iota(jnp.int32, logits.shape, 1)
    lane_f = lane.astype(F32)
    lg = jnp.where(lane < N_EXPERTS, logits, -jnp.inf)
    te = jnp.zeros(lg.shape, F32)
    tw = jnp.zeros(lg.shape, F32)
    top0 = None
    for kk in range(TOP_K):
        mx = jnp.max(lg, axis=-1, keepdims=True)
        idx = jnp.min(jnp.where(lg == mx, lane_f, 128.0), axis=-1, keepdims=True)
        if kk == 0:
            top0 = mx
        te = jnp.where(lane == kk, idx, te)
        tw = jnp.where(lane == kk, jnp.exp(mx - top0), tw)
        lg = jnp.where(lane_f == idx, -jnp.inf, lg)
    te_ref[...] = te.astype(jnp.int32)
    tw_ref[...] = tw / jnp.sum(tw, axis=-1, keepdims=True)


def _tail(x, yml, ymla, mods, w, tile, tiles_per_mod):
    n, d = x.shape
    nm, _, r, _ = mods.shape
    assert nm == 5
    const = lambda a: pl.BlockSpec(a.shape, lambda i: (0,) * a.ndim)
    tok = lambda width: pl.BlockSpec((tile, width), lambda i: (i, 0))
    names = ("g_pre_mix", "g_post_mix", "g_pre_ffn", "w_gate", "b_gate", "w_br_ml", "w_br_mla", "w_out",
             "w_router_hi", "w_router_lo", "b_router")
    return pl.pallas_call(
        _tail_body,
        grid=(n // tile,),
        in_specs=[tok(d), tok(d), tok(d),
                  pl.BlockSpec((5, 1, r, d), lambda i: (0, i // tiles_per_mod, 0, 0))]
                 + [const(w[k]) for k in names],
        out_specs=(tok(d), tok(d), tok(128), tok(128)),
        out_shape=(jax.ShapeDtypeStruct((n, d), F32), jax.ShapeDtypeStruct((n, d), F32),
                   jax.ShapeDtypeStruct((n, 128), jnp.int32), jax.ShapeDtypeStruct((n, 128), F32)),
        compiler_params=_cparams(("parallel",)), name="tail",
    )(x, yml, ymla, mods, *[w[k] for k in names])


def _moe_body(be_ref, jb_ref, nu_ref, src_ref, srcn_ref, dst_ref, h2_hbm, wup_ref, bup_ref, wdn_ref, bdn_ref,
              out_hbm, xbuf, obuf, sem_in, sem_out):
    del be_ref
    w = pl.program_id(0)
    n_used = nu_ref[0]
    rows = xbuf.shape[1]
    ff = wdn_ref.shape[1]

    def gather_start(idx_ref, slot):
        def issue(i, c):
            pltpu.make_async_copy(h2_hbm.at[pl.ds(idx_ref[0, 0, i], 1)], xbuf.at[slot, pl.ds(i, 1)],
                                  sem_in.at[slot]).start()
            return c
        lax.fori_loop(0, rows, issue, 0, unroll=8)

    def gather_wait(slot):
        pltpu.make_async_copy(h2_hbm.at[pl.ds(0, rows)], xbuf.at[slot], sem_in.at[slot]).wait()

    def scatter_start(slot):
        def issue(i, c):
            pltpu.make_async_copy(obuf.at[slot, pl.ds(i, 1)], out_hbm.at[pl.ds(dst_ref[0, 0, i], 1)],
                                  sem_out).start()
            return c
        lax.fori_loop(0, rows, issue, 0, unroll=8)

    def scatter_wait(slot):
        pltpu.make_async_copy(obuf.at[slot], out_hbm.at[pl.ds(0, rows)], sem_out).wait()

    @pl.when(w < n_used)
    def _():
        blk = jb_ref[w]
        slot = blk & 1
        oslot = w & 1
        first_of_block = jnp.logical_or(w == 0, blk != jb_ref[jnp.maximum(w - 1, 0)])
        last_of_block = jnp.logical_and(w + 1 < n_used, jb_ref[jnp.minimum(w + 1, n_used - 1)] != blk)

        @pl.when(w == 0)
        def _():
            gather_start(src_ref, slot)

        @pl.when(first_of_block)
        def _():
            gather_wait(slot)

        @pl.when(last_of_block)
        def _():
            gather_start(srcn_ref, 1 - slot)

        gu = _dot(xbuf[slot].astype(BF16), wup_ref[0]) + bup_ref[0]
        gl = jnp.minimum(gu[:, :ff], SWIGLU_LIMIT)
        up = jnp.clip(gu[:, ff:], -SWIGLU_LIMIT, SWIGLU_LIMIT)
        act = (up + 1.0) * gl * jax.nn.sigmoid(SWIGLU_ALPHA * gl)
        obuf[oslot] = _dot(act.astype(BF16), wdn_ref[0]) + bdn_ref[0]

        @pl.when(w > 0)
        def _():
            scatter_wait(1 - oslot)

        scatter_start(oslot)

        @pl.when(w == n_used - 1)
        def _():
            scatter_wait(oslot)


def _moe(item_e, item_blk, n_used, src_blk, dst_item, h2, w_up, b_up, w_down, b_down, out_rows):
    n_items = dst_item.shape[0]
    d = h2.shape[1]
    ff2 = w_up.shape[2]
    ff = w_down.shape[1]
    smem = lambda imap: pl.BlockSpec((1, 1, MOE_ROWS), imap, memory_space=pltpu.SMEM)
    wspec = lambda shape: pl.BlockSpec(shape, lambda w, be, jb, nu: (be[w], 0, 0))
    grid_spec = pltpu.PrefetchScalarGridSpec(
        num_scalar_prefetch=3,
        grid=(n_items,),
        in_specs=[smem(lambda w, be, jb, nu: (jb[w], 0, 0)),
                  smem(lambda w, be, jb, nu: (jb[jnp.minimum(w + 1, n_items - 1)], 0, 0)),
                  smem(lambda w, be, jb, nu: (w, 0, 0)),
                  pl.BlockSpec(memory_space=pl.ANY),
                  wspec((1, d, ff2)), wspec((1, 1, ff2)), wspec((1, ff, d)), wspec((1, 1, d))],
        out_specs=pl.BlockSpec(memory_space=pl.ANY),
        scratch_shapes=[pltpu.VMEM((2, MOE_ROWS, d), F32), pltpu.VMEM((2, MOE_ROWS, d), F32),
                        pltpu.SemaphoreType.DMA((2,)), pltpu.SemaphoreType.DMA],
    )
    return pl.pallas_call(
        _moe_body,
        grid_spec=grid_spec,
        out_shape=jax.ShapeDtypeStruct((out_rows, d), F32),
        compiler_params=_cparams(("arbitrary",)),
        name="moe",
    )(item_e, item_blk, n_used, src_blk, src_blk, dst_item, h2, w_up, b_up, w_down, b_down)


def _combine_body(o0_ref, o1_ref, o2_ref, o3_ref, tw_ref, x1_ref, mod_ref, g_ref, y_ref):
    tw = tw_ref[...]
    lane = lax.broadcasted_iota(jnp.int32, tw.shape, 1)
    acc = None
    for kk, o_ref in enumerate((o0_ref, o1_ref, o2_ref, o3_ref)):
        wk = jnp.sum(jnp.where(lane == kk, tw, 0.0), axis=-1, keepdims=True)
        term = o_ref[...] * wk
        acc = term if acc is None else acc + term
    y_ref[...] = x1_ref[...] + mod_ref[0, 0] * _rms(acc, g_ref[...])


def _combine(moe_out, tw, x1, mods, g_post_ffn, n_pad, row0, rows, tile, tiles_per_mod):
    d = x1.shape[1]
    r = mods.shape[2]
    assert row0 % tile == 0 and n_pad % tile == 0 and rows % tile == 0
    t0 = row0 // tile
    kt = n_pad // tile
    ospec = lambda kk: pl.BlockSpec((tile, d), lambda i: (kk * kt + t0 + i, 0))
    return pl.pallas_call(
        _combine_body,
        grid=(rows // tile,),
        in_specs=[ospec(0), ospec(1), ospec(2), ospec(3),
                  pl.BlockSpec((tile, 128), lambda i: (t0 + i, 0)),
                  pl.BlockSpec((tile, d), lambda i: (t0 + i, 0)),
                  pl.BlockSpec((1, 1, r, d), lambda i: (0, i // tiles_per_mod, 0, 0)),
                  pl.BlockSpec((1, d), lambda i: (0, 0))],
        out_specs=pl.BlockSpec((tile, d), lambda i: (i, 0)),
        out_shape=jax.ShapeDtypeStruct((rows, d), F32),
        compiler_params=_cparams(("parallel",)), name="combine",
    )(moe_out, moe_out, moe_out, moe_out, tw, x1, mods, g_post_ffn)


def _rope_table(pos):
    inv = 1.0 / (ROPE_THETA ** (jnp.arange(0, MLA_DROPE, 2, dtype=F32) / MLA_DROPE))
    ang = pos.astype(F32)[:, None] * inv[None, :]
    c, s = jnp.cos(ang), jnp.sin(ang)
    return jnp.concatenate([c, c, s, s], axis=-1)


def _rotate_cols(w):
    half = w.shape[-1] // 2
    return jnp.concatenate([-w[..., half:], w[..., :half]], axis=-1)


def _prep_weights(w_in, b_in, ml_f_bias, mla_g_q, w_uq, mla_g_kv, w_uk, w_uv, g_pre_mix):
    qk_w = ML_HEADS * ML_DQK
    v_w = ML_HEADS * ML_DV
    sizes = [qk_w, qk_w, v_w, ML_HEADS, ML_HEADS, v_w, MLA_Q_RANK, MLA_KV_RANK, MLA_DROPE]
    pts = [int(p) for p in np.cumsum(sizes)[:-1]]
    wq, wk, wv, wi, wf, wo, wql, wkv, wpe = jnp.split(w_in, pts, axis=-1)
    bq, bk, bv, bi, bf, bo, bql, bkv, bpe = jnp.split(b_in, pts, axis=-1)
    d = w_in.shape[0]
    pad_w = jnp.zeros((d, 128 - 2 * ML_HEADS), F32)
    pad_b = jnp.zeros((128 - 2 * ML_HEADS,), F32)
    w_small = jnp.concatenate([wql, wkv, wpe, _rotate_cols(wpe), wi, wf, pad_w], axis=-1)
    b_small = jnp.concatenate([bql, bkv, bpe, _rotate_cols(bpe), bi, bf, pad_b], axis=-1)
    f_bias = jnp.concatenate([jnp.zeros((ML_HEADS,), F32), ml_f_bias, pad_b], axis=-1)
    wuq_h = w_uq.reshape(MLA_Q_RANK, MLA_HEADS, MLA_DNOPE + MLA_DROPE)
    wuq_r = wuq_h[..., MLA_DNOPE:]
    wuq_ext = jnp.concatenate([wuq_h, _rotate_cols(wuq_r)], axis=-1).reshape(MLA_Q_RANK, MLA_HEADS * QK_PAD)
    return {
        "g_pre_mix": g_pre_mix[None, :],
        "w_main": jnp.concatenate([wq, wk, wv, wo], axis=-1).astype(BF16),
        "b_main": jnp.concatenate([bq, bk, bv, bo], axis=-1)[None, :],
        "w_small": w_small.astype(BF16), "b_small": b_small[None, :], "f_bias": f_bias[None, :],
        "g_q": mla_g_q[None, :], "g_kv": mla_g_kv[None, :],
        "w_uq": wuq_ext.astype(BF16), "w_uk": w_uk.astype(BF16), "w_uv": w_uv.astype(BF16),
        "w_uv_t": w_uv.reshape(MLA_KV_RANK, MLA_HEADS, MLA_DV).transpose(1, 2, 0).astype(BF16),
    }


def _route_items(te, n_tok, n_pad):
    n_assign = n_tok * TOP_K
    n_blocks = -(-n_assign // MOE_ROWS)
    n_items = n_blocks + N_EXPERTS - 1
    i32 = jnp.int32
    e_flat = te.reshape(-1)
    skey = jnp.sort(e_flat * (1 << ORDER_BITS) + jnp.arange(n_assign, dtype=i32))
    order = skey & ((1 << ORDER_BITS) - 1)
    start = jnp.sum(e_flat[None, :] < jnp.arange(N_EXPERTS + 1, dtype=i32)[:, None], axis=1).astype(i32)
    lo_e, hi_e = start[:-1], start[1:]
    first_blk = lo_e // MOE_ROWS
    n_it = jnp.where(hi_e > lo_e, (hi_e - 1) // MOE_ROWS - first_blk + 1, 0)
    it_end = jnp.cumsum(n_it)
    n_used = it_end[-1]
    w = jnp.minimum(jnp.arange(n_items, dtype=i32), n_used - 1)
    item_e = jnp.minimum(jnp.searchsorted(it_end, w, side="right", method="compare_all"), N_EXPERTS - 1).astype(i32)
    item_blk = first_blk[item_e] + w - (it_end - n_it)[item_e]
    r = jnp.arange(MOE_ROWS, dtype=i32)[None, :]
    pos = item_blk[:, None] * MOE_ROWS + r
    mine = jnp.logical_and(pos >= lo_e[item_e][:, None], pos < hi_e[item_e][:, None])
    order_p = jnp.pad(order, (0, n_blocks * MOE_ROWS - n_assign)).reshape(n_blocks, MOE_ROWS)
    src_blk = order_p >> 2
    dst_blk = (order_p & 3) * n_pad + src_blk
    dst_item = jnp.where(mine, dst_blk[item_blk], TOP_K * n_pad + r)
    return item_e, item_blk, n_used.reshape(1).astype(i32), src_blk[:, None, :], dst_item[:, None, :]


def kernel(x_prompt, x_sample, cache_kv_latent, cache_k_rope, state_mlstm_C, state_mlstm_n, state_mlstm_m,
           c_prompt, c_sample, w_ada, b_ada, g_pre_mix, g_post_mix, g_pre_ffn, g_post_ffn, w_in, b_in,
           ml_f_bias, ml_head_g, mla_g_q, w_uq, mla_g_kv, w_uk, w_uv, w_br_ml, w_br_mla, w_gate, b_gate,
           w_out, w_router, b_router, w_up, b_up, w_down, b_down):
    assert w_ada.shape[0] == 1, "single-layer step"
    bp, sp, d = x_prompt.shape
    bs, ss, _ = x_sample.shape
    past = cache_kv_latent.shape[2]
    n_p, n_s = bp * sp, bs * ss
    n_tok = n_p + n_s
    tile_p = TOK_TILE
    tile_s = min(TOK_TILE, n_s)
    n_pad = -(-n_tok // tile_p) * tile_p
    assert n_tok * TOP_K < (1 << ORDER_BITS)
    assert sp % ATT_TILE == 0 and sp % ML_CHUNK == 0 and sp % tile_p == 0
    assert n_s % tile_s == 0 and n_p % tile_s == 0 and n_pad % tile_s == 0

    mods = _ada(jnp.concatenate([c_prompt, c_sample], axis=0), w_ada[0], b_ada[0][None, :])
    mods = mods.reshape(bp + bs, 6, d).transpose(1, 0, 2)
    mods_p = mods[:, :bp][:, :, None, :]
    mods_s = jnp.broadcast_to(mods[:, bp:, None, :], (6, bs, ss, d)).reshape(6, n_s // tile_s, tile_s, d)

    wm = _prep_weights(w_in[0], b_in[0], ml_f_bias[0], mla_g_q[0], w_uq[0], mla_g_kv[0], w_uk[0], w_uv[0],
                       g_pre_mix[0])
    cs_p = _rope_table(jnp.arange(sp))
    cs_s = jnp.tile(_rope_table(past + jnp.arange(ss)), (bs, 1))

    xp = x_prompt.reshape(n_p, d)
    xs = x_sample.reshape(n_s, d)
    tiles_per_stream = sp // tile_p
    mp = _mixer_in(xp, mods_p[:2], cs_p, wm, tile_p, tiles_per_stream, tiles_per_stream)
    ms = _mixer_in(xs, mods_s[:2], cs_s, wm, tile_s, 1, n_s // tile_s)

    g_head = ml_head_g[0][None, :]

    def gate_rows(g, b, s):
        return g[:, :2 * ML_HEADS].reshape(b, s, 2 * ML_HEADS).transpose(0, 2, 1)

    qp, kp, vp, op, gp, lat_p, kr_p, qcat_p, kcat_p, vv_p = mp
    zc = jnp.zeros((bp, ML_HEADS, ML_DQK, ML_DV), F32)
    zn = jnp.zeros((bp, ML_HEADS, 1, ML_DQK), F32)
    zm = jnp.zeros((bp, ML_HEADS, 1, 1), F32)
    yml_p, c_p, nn_p, m_p = _mlstm(qp, kp, vp, op, gp, gate_rows(gp, bp, sp), zc, zn, zm, g_head, ML_CHUNK)
    ymla_p = _flash(qcat_p, kcat_p, vv_p, bp, sp, ATT_TILE)

    qs, ks, vs, os_, gs, lat_s, kr_s, qcat_s, kcat_s, _ = ms
    yml_s, c_s, nn_s, m_s = _mlstm(qs, ks, vs, os_, gs, gate_rows(gs, bs, ss),
                                   state_mlstm_C[0], state_mlstm_n[0][:, :, None, :],
                                   state_mlstm_m[0][:, :, None, None], g_head, ss)
    ckr_pad = jnp.pad(cache_k_rope[0], ((0, 0), (0, 0), (0, 128 - MLA_DROPE)))
    ymla_s = _sattn(qcat_s, cache_kv_latent[0], lat_s, ckr_pad, kcat_s, wm["w_uk"], wm["w_uv"], ss)

    w_r = jnp.pad(w_router[0], ((0, 0), (0, 128 - N_EXPERTS)))
    w_r_hi = w_r.astype(BF16)
    wt = {
        "g_pre_mix": g_pre_mix[0][None, :], "g_post_mix": g_post_mix[0][None, :], "g_pre_ffn": g_pre_ffn[0][None, :],
        "w_gate": w_gate[0].astype(BF16), "b_gate": b_gate[0][None, :],
        "w_br_ml": w_br_ml[0].astype(BF16), "w_br_mla": w_br_mla[0].astype(BF16), "w_out": w_out[0].astype(BF16),
        "w_router_hi": w_r_hi, "w_router_lo": (w_r - w_r_hi.astype(F32)).astype(BF16),
        "b_router": jnp.pad(b_router[0], (0, 128 - N_EXPERTS))[None, :],
    }
    x1_p, h2_p, te_p, tw_p = _tail(xp, yml_p, ymla_p, mods_p[:5], wt, tile_p, tiles_per_stream)
    x1_s, h2_s, te_s, tw_s = _tail(xs, yml_s, ymla_s, mods_s[:5], wt, tile_s, 1)

    x1 = jnp.concatenate([x1_p, x1_s], axis=0)
    h2 = jnp.concatenate([h2_p, h2_s], axis=0)
    te = jnp.concatenate([te_p, te_s], axis=0)[:, :TOP_K]
    tw = jnp.concatenate([tw_p, tw_s], axis=0)

    item_e, item_blk, n_used, src_blk, dst_item = _route_items(te, n_tok, n_pad)
    moe_out = _moe(item_e, item_blk, n_used, src_blk, dst_item, h2, w_up[0].astype(BF16), b_up[0][:, None, :],
                   w_down[0].astype(BF16), b_down[0][:, None, :], n_pad * TOP_K + MOE_ROWS)
    gff = g_post_ffn[0][None, :]
    y_p = _combine(moe_out, tw, x1, mods_p[5:], gff, n_pad, 0, n_p, tile_p, tiles_per_stream)
    y_s = _combine(moe_out, tw, x1, mods_s[5:], gff, n_pad, n_p, n_s, tile_s, 1)

    return (y_p.reshape(bp, sp, d), y_s.reshape(bs, ss, d),
            lat_p.reshape(1, bp, sp, MLA_KV_RANK), kr_p.reshape(1, bp, sp, MLA_DROPE),
            c_p[None], nn_p.reshape(1, bp, ML_HEADS, ML_DQK), m_p.reshape(1, bp, ML_HEADS),
            lat_s.reshape(1, bs, ss, MLA_KV_RANK), kr_s.reshape(1, bs, ss, MLA_DROPE),
            c_s[None], nn_s.reshape(1, bs, ML_HEADS, ML_DQK), m_s.reshape(1, bs, ML_HEADS))
```

```python
import functools

import jax
import jax.numpy as jnp
import numpy as np
from jax import lax
from jax.experimental import pallas as pl
from jax.experimental.pallas import tpu as pltpu

F32 = jnp.float32
BF16 = jnp.bfloat16

EPS = 1e-6
ROPE_THETA = 10000.0
SWIGLU_ALPHA = 1.702
SWIGLU_LIMIT = 7.0
TOP_K = 4

ML_HEADS = 4
ML_DQK = 128
ML_DV = 256
MLA_HEADS = 8
MLA_DNOPE = 128
MLA_DROPE = 64
MLA_DV = 128
MLA_Q_RANK = 384
MLA_KV_RANK = 256
MLA_SCALE = (MLA_DNOPE + MLA_DROPE) ** -0.5
Q_SCALE = MLA_SCALE * float(np.log2(np.e))
QK_PAD = 256
N_EXPERTS = 32

VMEM_LIMIT_BYTES = 52 * 1024 * 1024

TOK_TILE = 512
ML_CHUNK = 256
ATT_TILE = 512
FLASH_SPLIT = 1
FLASH_KCHUNK = ATT_TILE
FLASH_UNROLL = 2
STREAM_CHUNK = 64
MOE_ROWS = 512
TOKEN_TILE_ROWS = 8
ORDER_BITS = 18


def _cparams(sem, vmem=VMEM_LIMIT_BYTES):
    return pltpu.CompilerParams(dimension_semantics=sem, vmem_limit_bytes=vmem)


def _resident(a):
    return pl.BlockSpec(a.shape, lambda i: (0,) * a.ndim, pipeline_mode=pl.Buffered(1))


def _dot(a, b):
    return jnp.dot(a, b, preferred_element_type=F32)


def _dot_nt(a, b):
    return lax.dot_general(a, b, (((1,), (1,)), ((), ())), preferred_element_type=F32)


def _rms(x, g):
    return x * lax.rsqrt(jnp.mean(x * x, axis=-1, keepdims=True) + EPS) * g


def _store_token_tiles(ref, x):
    t = x.shape[0]
    for s in range(TOKEN_TILE_ROWS):
        ref[pl.ds(s, t, stride=TOKEN_TILE_ROWS), :] = x[:, s * 128:(s + 1) * 128]


def _load_token_tiles(ref, t):
    return [ref[pl.ds(s, t, stride=TOKEN_TILE_ROWS), :] for s in range(TOKEN_TILE_ROWS)]


def _split3(x):
    hi = x.astype(BF16)
    r1 = x - hi.astype(F32)
    mid = r1.astype(BF16)
    lo = (r1 - mid.astype(F32)).astype(BF16)
    return hi, mid, lo


def _ada_body(c_ref, w_ref, b_ref, o_ref):
    o_ref[...] = _dot(c_ref[...].astype(BF16), w_ref[...].astype(BF16)) + b_ref[...]


def _ada(c, w, b):
    m, d = c.shape
    n = w.shape[1]
    return pl.pallas_call(
        _ada_body,
        grid=(n // d,),
        in_specs=[pl.BlockSpec((m, d), lambda j: (0, 0)),
                  pl.BlockSpec((d, d), lambda j: (0, j)),
                  pl.BlockSpec((1, d), lambda j: (0, j))],
        out_specs=pl.BlockSpec((m, d), lambda j: (0, j)),
        out_shape=jax.ShapeDtypeStruct((m, n), F32),
        compiler_params=_cparams(("parallel",)), name="ada",
    )(c, w, b)


_SM_QLAT = 0
_SM_KVLAT = MLA_Q_RANK
_SM_KPE = MLA_Q_RANK + MLA_KV_RANK
_SM_GATE = _SM_KPE + 128
_SM_W = _SM_GATE + 128


def _mixer_in_body(x_ref, mod_ref, gpre_ref, wm_ref, bm_ref, ws_ref, bs_ref, fb_ref, gq_ref, gkv_ref,
                   wuq_ref, wuk_ref, wuvt_ref, cs_ref,
                   q_ref, k_ref, v_ref, o_ref, gate_ref, lat_ref, kr_ref, qcat_ref, kcat_ref, vvt_ref):
    x = x_ref[...]
    shift = mod_ref[0, 0]
    scale = mod_ref[1, 0]
    h = _rms(x, gpre_ref[...]) * (1.0 + scale) + shift
    hb = h.astype(BF16)
    qk_w = ML_HEADS * ML_DQK
    v_w = ML_HEADS * ML_DV
    q_ref[...] = (_dot(hb, wm_ref[:, 0:qk_w]) + bm_ref[:, 0:qk_w]).astype(BF16)
    k_ref[...] = ((_dot(hb, wm_ref[:, qk_w:2 * qk_w]) + bm_ref[:, qk_w:2 * qk_w]) * (ML_DQK ** -0.5)).astype(BF16)
    v_ref[...] = (_dot(hb, wm_ref[:, 2 * qk_w:2 * qk_w + v_w]) + bm_ref[:, 2 * qk_w:2 * qk_w + v_w]).astype(BF16)
    o_ref[...] = (_dot(hb, wm_ref[:, 2 * qk_w + v_w:]) + bm_ref[:, 2 * qk_w + v_w:]).astype(BF16)

    zs = _dot(hb, ws_ref[...]) + bs_ref[...]
    g = zs[:, _SM_GATE:_SM_GATE + 128]
    gf = g + fb_ref[...]
    logsig = jnp.minimum(gf, 0.0) - jnp.log1p(jnp.exp(-jnp.abs(gf)))
    lane = lax.broadcasted_iota(jnp.int32, g.shape, 1)
    gate_ref[...] = jnp.where(lane >= ML_HEADS, logsig, g)

    cs = cs_ref[...]
    qln = _rms(zs[:, _SM_QLAT:_SM_QLAT + MLA_Q_RANK], gq_ref[...]).astype(BF16)
    cq = _dot(qln, wuq_ref[...])
    for hd in range(MLA_HEADS):
        blk = cq[:, hd * QK_PAD:(hd + 1) * QK_PAD]
        t = blk[:, MLA_DNOPE:] * cs
        rot = t + pltpu.roll(t, 64, 1)
        qcat_ref[hd, :, 0:MLA_DNOPE] = (blk[:, :MLA_DNOPE] * Q_SCALE).astype(BF16)
        qcat_ref[hd, :, MLA_DNOPE:] = (rot * Q_SCALE).astype(BF16)
    latent = _rms(zs[:, _SM_KVLAT:_SM_KVLAT + MLA_KV_RANK], gkv_ref[...])
    lat_ref[...] = latent
    tk = zs[:, _SM_KPE:_SM_KPE + 128] * cs
    rotk = tk + pltpu.roll(tk, 64, 1)
    kr_ref[...] = rotk[:, 0:MLA_DROPE]
    krz = jnp.where(lane < MLA_DROPE, rotk, 0.0).astype(BF16)
    latb = latent.astype(BF16)
    kn = _dot(latb, wuk_ref[...])
    for hd in range(MLA_HEADS):
        kcat_ref[hd, :, 0:MLA_DNOPE] = kn[:, hd * MLA_DNOPE:(hd + 1) * MLA_DNOPE].astype(BF16)
        kcat_ref[hd, :, MLA_DNOPE:] = krz
        vvt_ref[hd] = _dot_nt(wuvt_ref[hd], latb).astype(BF16)


def _mixer_in(x, mods, cs, w, tile, tiles_per_mod, cs_tiles):
    n, d = x.shape
    nm, _, r, _ = mods.shape
    grid = (n // tile,)
    const = _resident
    tok = lambda width: pl.BlockSpec((tile, width), lambda i: (i, 0))
    head = lambda width: pl.BlockSpec((MLA_HEADS, tile, width), lambda i: (0, i, 0))
    qk_w = ML_HEADS * ML_DQK
    v_w = ML_HEADS * ML_DV
    out_shape = (
        jax.ShapeDtypeStruct((n, qk_w), BF16), jax.ShapeDtypeStruct((n, qk_w), BF16),
        jax.ShapeDtypeStruct((n, v_w), BF16), jax.ShapeDtypeStruct((n, v_w), BF16),
        jax.ShapeDtypeStruct((n, 128), F32),
        jax.ShapeDtypeStruct((n, MLA_KV_RANK), F32), jax.ShapeDtypeStruct((n, MLA_DROPE), F32),
        jax.ShapeDtypeStruct((MLA_HEADS, n, QK_PAD), BF16), jax.ShapeDtypeStruct((MLA_HEADS, n, QK_PAD), BF16),
        jax.ShapeDtypeStruct((MLA_HEADS, MLA_DV, n), BF16),
    )
    out_specs = (tok(qk_w), tok(qk_w), tok(v_w), tok(v_w), tok(128), tok(MLA_KV_RANK), tok(MLA_DROPE),
                 head(QK_PAD), head(QK_PAD), pl.BlockSpec((MLA_HEADS, MLA_DV, tile), lambda i: (0, 0, i)))
    in_specs = [
        tok(d),
        pl.BlockSpec((nm, 1, r, d), lambda i: (0, i // tiles_per_mod, 0, 0)),
        const(w["g_pre_mix"]), const(w["w_main"]), const(w["b_main"]), const(w["w_small"]), const(w["b_small"]),
        const(w["f_bias"]), const(w["g_q"]), const(w["g_kv"]), const(w["w_uq"]), const(w["w_uk"]), const(w["w_uv_t"]),
        pl.BlockSpec((tile, 128), lambda i: (i % cs_tiles, 0)),
    ]
    return pl.pallas_call(
        _mixer_in_body, grid=grid, in_specs=in_specs, out_specs=out_specs, out_shape=out_shape,
        compiler_params=_cparams(("parallel",)), name="mixer_in",
    )(x, mods, w["g_pre_mix"], w["w_main"], w["b_main"], w["w_small"], w["b_small"], w["f_bias"],
      w["g_q"], w["g_kv"], w["w_uq"], w["w_uk"], w["w_uv_t"], cs)


def _mlstm_body(q_ref, k_ref, v_ref, o_ref, gc_ref, gr_ref, c0_ref, n0_ref, m0_ref, gh_ref,
                y_ref, c_ref, n_ref, m_ref, br_ref):
    chunk = q_ref.shape[0]

    @pl.when(pl.program_id(1) == 0)
    def _():
        c_ref[...] = c0_ref[...]
        n_ref[...] = n0_ref[...]
        m_ref[...] = m0_ref[...]

    row = lax.broadcasted_iota(jnp.int32, (chunk, chunk), 0)
    col = lax.broadcasted_iota(jnp.int32, (chunk, chunk), 1)
    causal = col <= row
    tri = causal.astype(BF16)
    tri_t = (row <= col).astype(BF16)

    gc = gc_ref[...]
    gr = gr_ref[0]
    bc_all = sum(_dot(tri, p) for p in _split3(gc))
    br_ref[...] = sum(_dot(p, tri_t) for p in _split3(gr))

    lane = lax.broadcasted_iota(jnp.int32, gc.shape, 1)
    rowv = lax.broadcasted_iota(jnp.int32, (chunk, 1), 0)
    is_last = rowv == chunk - 1

    def pick(a, j):
        return jnp.sum(jnp.where(lane == j, a, 0.0), axis=-1, keepdims=True)

    def last(a):
        return jnp.sum(jnp.where(is_last, a, 0.0), axis=0, keepdims=True)

    for hd in range(ML_HEADS):
        qs = slice(hd * ML_DQK, (hd + 1) * ML_DQK)
        vs = slice(hd * ML_DV, (hd + 1) * ML_DV)
        q = q_ref[:, qs]
        k = k_ref[:, qs]
        v = v_ref[:, vs]
        ig_c = pick(gc, hd)
        b_c = pick(bc_all, ML_HEADS + hd)
        ig_r = gr_ref[0, hd:hd + 1, :]
        b_r = br_ref[ML_HEADS + hd:ML_HEADS + hd + 1, :]
        m_prev = m_ref[0, hd]
        c_prev = c_ref[0, hd]
        n_prev = n_ref[0, hd]

        dmat = jnp.where(causal, b_c - b_r + ig_r, -jnp.inf)
        inter = b_c + m_prev
        m_t = jnp.maximum(inter, jnp.max(dmat, axis=-1, keepdims=True))
        w_intra = jnp.exp(dmat - m_t)
        w_inter = jnp.exp(inter - m_t)
        qk = _dot_nt(q, k) * w_intra
        num = w_inter * _dot(q, c_prev.astype(BF16)) + _dot(qk.astype(BF16), v)
        qn = jnp.sum(q.astype(F32) * n_prev, axis=-1, keepdims=True)
        den = w_inter * qn + jnp.sum(qk, axis=-1, keepdims=True)
        hcur = num / jnp.maximum(jnp.abs(den), jnp.exp(-m_t))
        hn = hcur * lax.rsqrt(jnp.mean(hcur * hcur, axis=-1, keepdims=True) + EPS)
        y_ref[:, vs] = (hn * gh_ref[:, vs] * jax.nn.sigmoid(o_ref[:, vs].astype(F32))).astype(BF16)

        m_new = last(m_t)
        w_prev = jnp.exp(last(inter) - m_new)
        w_end = jnp.exp(last(b_c) - b_c + ig_c - m_new)
        kw = k.astype(F32) * w_end
        c_ref[0, hd] = w_prev * c_prev + _dot(kw.T.astype(BF16), v)
        n_ref[0, hd] = w_prev * n_prev + jnp.sum(kw, axis=0, keepdims=True)
        m_ref[0, hd] = m_new


def _mlstm(q, k, v, o, gates_col, gates_row, c0, n0, m0, g_head, chunk):
    n = q.shape[0]
    b, _, s = gates_row.shape
    nc = s // chunk
    tok = lambda width: pl.BlockSpec((chunk, width), lambda i, c: (i * nc + c, 0))
    state = lambda a: pl.BlockSpec((1,) + a.shape[1:], lambda i, c: (i,) + (0,) * (a.ndim - 1))
    qk_w = ML_HEADS * ML_DQK
    v_w = ML_HEADS * ML_DV
    return pl.pallas_call(
        _mlstm_body,
        grid=(b, nc),
        in_specs=[tok(qk_w), tok(qk_w), tok(v_w), tok(v_w), tok(128),
                  pl.BlockSpec((1, 8, chunk), lambda i, c: (i, 0, c)),
                  state(c0), state(n0), state(m0),
                  pl.BlockSpec((1, v_w), lambda i, c: (0, 0))],
        out_specs=(tok(v_w), state(c0), state(n0), state(m0)),
        out_shape=(jax.ShapeDtypeStruct((n, v_w), BF16),
                   jax.ShapeDtypeStruct(c0.shape, F32), jax.ShapeDtypeStruct(n0.shape, F32),
                   jax.ShapeDtypeStruct(m0.shape, F32)),
        scratch_shapes=[pltpu.VMEM((8, chunk), F32)],
        compiler_params=_cparams(("parallel", "arbitrary")), name="mlstm",
    )(q, k, v, o, gates_col, gates_row, c0, n0, m0, g_head)


def _flash_body(q_ref, k_ref, vt_ref, o_ref, m_sc, l_sc, acc_sc):
    tile = q_ref.shape[1]
    half = tile // FLASH_SPLIT
    qi = pl.program_id(2)
    m_sc[...] = jnp.full(m_sc.shape, -jnp.inf, F32)
    l_sc[...] = jnp.zeros(l_sc.shape, F32)
    acc_sc[...] = jnp.zeros(acc_sc.shape, F32)

    def step(j, masked):
        off = pl.multiple_of(j * tile, tile)
        for part in range(FLASH_SPLIT):
            qs = slice(part * half, (part + 1) * half)
            qpart = q_ref[0, qs, :]
            m_run, l_run, acc = m_sc[:, qs], l_sc[:, qs], acc_sc[:, qs]
            for c in range(tile // FLASH_KCHUNK):
                k0 = c * FLASH_KCHUNK
                if masked and k0 >= (part + 1) * half:
                    continue
                kt = k_ref[0, pl.ds(off + k0, FLASH_KCHUNK), :]
                vt = vt_ref[0, :, pl.ds(off + k0, FLASH_KCHUNK)]
                st = _dot_nt(kt, qpart)
                if masked and k0 + FLASH_KCHUNK > part * half:
                    shift = STREAM_CHUNK.bit_length() - 1
                    kc = (lax.broadcasted_iota(jnp.int32, st.shape, 0) + k0) >> shift
                    qc = (lax.broadcasted_iota(jnp.int32, st.shape, 1) + part * half) >> shift
                    st = jnp.where(kc <= qc, st, -jnp.inf)
                m_new = jnp.maximum(m_run, jnp.max(st, axis=0, keepdims=True))
                alpha = jnp.exp2(m_run - m_new)
                p = jnp.exp2(st - m_new)
                l_run = alpha * l_run + jnp.sum(p, axis=0, keepdims=True)
                acc = alpha * acc + _dot(vt, p.astype(BF16))
                m_run = m_new
            m_sc[:, qs], l_sc[:, qs], acc_sc[:, qs] = m_run, l_run, acc

    def body(i, carry):
        for u in range(FLASH_UNROLL):
            step(FLASH_UNROLL * i + u, False)
        return carry

    def rest(j, carry):
        step(j, False)
        return carry

    n_full = qi // FLASH_UNROLL
    lax.fori_loop(0, n_full, body, 0)
    lax.fori_loop(n_full * FLASH_UNROLL, qi, rest, 0)
    step(qi, True)
    o_ref[...] = (acc_sc[...] / l_sc[...]).T.astype(BF16)


def _flash(qcat, kcat, vvt, batch, seq, tile):
    heads, n, _ = qcat.shape
    nq = seq // tile
    return pl.pallas_call(
        _flash_body,
        grid=(batch, heads, nq),
        in_specs=[pl.BlockSpec((1, tile, QK_PAD), lambda b, h, i: (h, b * nq + i, 0)),
                  pl.BlockSpec((1, seq, QK_PAD), lambda b, h, i: (h, b, 0)),
                  pl.BlockSpec((1, MLA_DV, seq), lambda b, h, i: (h, 0, b))],
        out_specs=pl.BlockSpec((tile, MLA_DV), lambda b, h, i: (b * nq + i, h)),
        out_shape=jax.ShapeDtypeStruct((n, heads * MLA_DV), BF16),
        scratch_shapes=[pltpu.VMEM((1, tile), F32), pltpu.VMEM((1, tile), F32), pltpu.VMEM((MLA_DV, tile), F32)],
        compiler_params=_cparams(("parallel", "parallel", "arbitrary")),
        name="flash",
    )(qcat, kcat, vvt)


def _sattn_body(q_ref, clat_ref, nlat_ref, ckr_ref, nk_ref, wuk_ref, wuv_ref, y_ref, qa_sc, qr_sc):
    s_new = nlat_ref.shape[0]
    for hd in range(MLA_HEADS):
        qh = q_ref[hd]
        hs = slice(hd * MLA_DNOPE, (hd + 1) * MLA_DNOPE)
        qa_sc[hd * s_new:(hd + 1) * s_new, :] = _dot_nt(qh[:, 0:MLA_DNOPE], wuk_ref[:, hs]).astype(BF16)
        qr_sc[hd * s_new:(hd + 1) * s_new, :] = qh[:, MLA_DNOPE:]
    qa = qa_sc[...]
    qr = qr_sc[...]
    clat = clat_ref[0].astype(BF16)
    nlat = nlat_ref[...].astype(BF16)
    ckr = ckr_ref[0].astype(BF16)
    nkr = nk_ref[0, :, MLA_DNOPE:]
    s_c = _dot_nt(qa, clat) + _dot_nt(qr, ckr)
    s_n = _dot_nt(qa, nlat) + _dot_nt(qr, nkr)
    m = jnp.maximum(jnp.max(s_c, axis=-1, keepdims=True), jnp.max(s_n, axis=-1, keepdims=True))
    p_c = jnp.exp2(s_c - m)
    p_n = jnp.exp2(s_n - m)
    l = jnp.sum(p_c, axis=-1, keepdims=True) + jnp.sum(p_n, axis=-1, keepdims=True)
    o_lat = ((_dot(p_c.astype(BF16), clat) + _dot(p_n.astype(BF16), nlat)) / l).astype(BF16)
    for hd in range(MLA_HEADS):
        hs = slice(hd * MLA_DV, (hd + 1) * MLA_DV)
        y_ref[:, hs] = _dot(o_lat[hd * s_new:(hd + 1) * s_new, :], wuv_ref[:, hs]).astype(BF16)


def _sattn(qcat, cache_lat, new_lat, cache_kr_pad, kcat, w_uk, w_uv, s_new):
    heads, n, _ = qcat.shape
    b, past, _ = cache_lat.shape
    const = _resident
    return pl.pallas_call(
        _sattn_body,
        grid=(b,),
        in_specs=[pl.BlockSpec((heads, s_new, QK_PAD), lambda i: (0, i, 0)),
                  pl.BlockSpec((1, past, MLA_KV_RANK), lambda i: (i, 0, 0)),
                  pl.BlockSpec((s_new, MLA_KV_RANK), lambda i: (i, 0)),
                  pl.BlockSpec((1, past, 128), lambda i: (i, 0, 0)),
                  pl.BlockSpec((1, s_new, QK_PAD), lambda i: (0, i, 0)),
                  const(w_uk), const(w_uv)],
        out_specs=pl.BlockSpec((s_new, heads * MLA_DV), lambda i: (i, 0)),
        out_shape=jax.ShapeDtypeStruct((n, heads * MLA_DV), BF16),
        scratch_shapes=[pltpu.VMEM((heads * s_new, MLA_KV_RANK), BF16), pltpu.VMEM((heads * s_new, 128), BF16)],
        compiler_params=_cparams(("parallel",)), name="sattn",
    )(qcat, cache_lat, new_lat, cache_kr_pad, kcat, w_uk, w_uv)


def _tail_body(x_ref, yml_ref, ymla_ref, mod_ref, gpre_ref, gpost_ref, gffn_ref, wg_ref, bg_ref,
               wml_ref, wmla_ref, wout_ref, wrh_ref, wrl_ref, br_ref,
               x1_ref, h2_ref, te_ref, tw_ref):
    x = x_ref[...]
    d = x.shape[1]
    sh1, sc1, gt1, sh2, sc2 = (mod_ref[j, 0] for j in range(5))
    hb = (_rms(x, gpre_ref[...]) * (1.0 + sc1) + sh1).astype(BF16)
    g = jax.nn.sigmoid(_dot(hb, wg_ref[...]) + bg_ref[...])
    yb = g[:, :d] * _dot(yml_ref[...], wml_ref[...]) + g[:, d:] * _dot(ymla_ref[...], wmla_ref[...])
    y = _dot(yb.astype(BF16), wout_ref[...])
    x1 = x + gt1 * _rms(y, gpost_ref[...])
    x1_ref[...] = x1
    h2 = _rms(x1, gffn_ref[...]) * (1.0 + sc2) + sh2
    _store_token_tiles(h2_ref, h2)
    hi = h2.astype(BF16)
    lo = (h2 - hi.astype(F32)).astype(BF16)
    logits = _dot(hi, wrh_ref[...]) + _dot(hi, wrl_ref[...]) + _dot(lo, wrh_ref[...]) + br_ref[...]
    lane = lax.broadcasted_iota(jnp.int32, logits.shape, 1)
    lane_f = lane.astype(F32)
    lg = jnp.where(lane < N_EXPERTS, logits, -jnp.inf)
    te = jnp.zeros(lg.shape, F32)
    tw = jnp.zeros(lg.shape, F32)
    top0 = None
    for kk in range(TOP_K):
        mx = jnp.max(lg, axis=-1, keepdims=True)
        idx = jnp.min(jnp.where(lg == mx, lane_f, 128.0), axis=-1, keepdims=True)
        if kk == 0:
            top0 = mx
        te = jnp.where(lane == kk, idx, te)
        tw = jnp.where(lane == kk, jnp.exp(mx - top0), tw)
        lg = jnp.where(lane_f == idx, -jnp.inf, lg)
    te_ref[...] = te.astype(jnp.int32)
    tw_ref[...] = tw / jnp.sum(tw, axis=-1, keepdims=True)


def _tail(x, yml, ymla, mods, w, tile, tiles_per_mod, n_total, row0=0, into=None):
    n, d = x.shape
    nm, _, r, _ = mods.shape
    assert nm == 5 and row0 % tile == 0
    t0 = row0 // tile
    const = _resident
    tok = lambda width: pl.BlockSpec((tile, width), lambda i: (i, 0))
    out = lambda rows, width: pl.BlockSpec((rows, width), lambda i: (t0 + i, 0))
    names = ("g_pre_mix", "g_post_mix", "g_pre_ffn", "w_gate", "b_gate", "w_br_ml", "w_br_mla", "w_out",
             "w_router_hi", "w_router_lo", "b_router")
    n_in = 4 + len(names)
    extra = () if into is None else tuple(into)

    def body(*refs):
        _tail_body(*refs[:n_in], *refs[n_in + len(extra):])

    return pl.pallas_call(
        body,
        grid=(n // tile,),
        in_specs=[tok(d), tok(d), tok(d),
                  pl.BlockSpec((5, 1, r, d), lambda i: (0, i // tiles_per_mod, 0, 0))]
                 + [const(w[k]) for k in names] + [pl.BlockSpec(memory_space=pl.ANY)] * len(extra),
        out_specs=(out(tile, d), out(tile * TOKEN_TILE_ROWS, 128), out(tile, 128), out(tile, 128)),
        out_shape=(jax.ShapeDtypeStruct((n_total, d), F32),
                   jax.ShapeDtypeStruct((n_total * TOKEN_TILE_ROWS, 128), F32),
                   jax.ShapeDtypeStruct((n_total, 128), jnp.int32), jax.ShapeDtypeStruct((n_total, 128), F32)),
        input_output_aliases={n_in + j: j for j in range(len(extra))},
        compiler_params=_cparams(("parallel",)), name="tail",
    )(x, yml, ymla, mods, *[w[k] for k in names], *extra)


def _moe_body(be_ref, jb_ref, nu_ref, src_ref, srcn_ref, dstp_ref, dst_ref, h2_hbm, wup_ref, bup_ref, wdn_ref,
              bdn_ref, out_hbm, xbuf0, xbuf1, obuf0, obuf1, sem_in, sem_out):
    del be_ref, jb_ref
    w = pl.program_id(0)
    n_used = nu_ref[0]
    rows = src_ref.shape[2]
    tr = TOKEN_TILE_ROWS
    ff = wdn_ref.shape[1]

    xbufs, obufs = (xbuf0, xbuf1), (obuf0, obuf1)

    def gather_copy(idx_ref, slot, i):
        return pltpu.make_async_copy(h2_hbm.at[pl.ds(pl.multiple_of(idx_ref[0, 0, i], tr), tr)],
                                     xbufs[slot].at[pl.ds(i * tr, tr)], sem_in.at[slot])

    def scatter_copy(idx_ref, slot, i):
        return pltpu.make_async_copy(obufs[slot].at[pl.ds(i * tr, tr)],
                                     out_hbm.at[pl.ds(pl.multiple_of(idx_ref[0, 0, i], tr), tr)], sem_out)

    def gather_wait(slot):
        pltpu.make_async_copy(h2_hbm.at[pl.ds(0, rows * tr)], xbufs[slot], sem_in.at[slot]).wait()

    def scatter_wait(slot):
        pltpu.make_async_copy(obufs[slot], out_hbm.at[pl.ds(0, rows * tr)], sem_out).wait()

    def rolled(copy, idx_ref, slot):
        def issue(i, c):
            copy(idx_ref, slot, i).start()
            return c
        lax.fori_loop(0, rows, issue, 0, unroll=8)

    def item(slot):
        other = 1 - slot
        @pl.when(w == 0)
        def _():
            rolled(gather_copy, src_ref, slot)
            obufs[other][...] = jnp.zeros(obufs[other].shape, F32)

        gather_wait(slot)

        @pl.when(w > 0)
        def _():
            scatter_wait(slot)

        x = jnp.concatenate(_load_token_tiles(xbufs[slot], rows), axis=-1).astype(BF16)
        for i in range(rows):
            gather_copy(srcn_ref, other, i).start()
            scatter_copy(dstp_ref, other, i).start()
        gu = _dot(x, wup_ref[0]) + bup_ref[0]
        gl = jnp.minimum(gu[:, :ff], SWIGLU_LIMIT)
        up = jnp.clip(gu[:, ff:], -SWIGLU_LIMIT, SWIGLU_LIMIT)
        act = (up + 1.0) * gl * jax.nn.sigmoid(SWIGLU_ALPHA * gl)
        _store_token_tiles(obufs[slot], _dot(act.astype(BF16), wdn_ref[0]) + bdn_ref[0])

        @pl.when(w == n_used - 1)
        def _():
            gather_wait(other)
            scatter_wait(other)
            rolled(scatter_copy, dst_ref, slot)
            scatter_wait(slot)

    for slot in (1, 0):
        pl.when(jnp.logical_and(w < n_used, (w & 1) == slot))(functools.partial(item, slot))


def _moe(item_e, item_blk, n_used, src_blk, dst_shift, h2t, w_up, b_up, w_down, b_down, out_rows):
    n_items = dst_shift.shape[0] - 1
    d = w_up.shape[1]
    ff2 = w_up.shape[2]
    ff = w_down.shape[1]
    tr = TOKEN_TILE_ROWS
    smem = lambda imap: pl.BlockSpec((1, 1, MOE_ROWS), imap, memory_space=pltpu.SMEM)
    wspec = lambda shape: pl.BlockSpec(shape, lambda w, be, jb, nu: (be[w], 0, 0))
    grid_spec = pltpu.PrefetchScalarGridSpec(
        num_scalar_prefetch=3,
        grid=(n_items,),
        in_specs=[smem(lambda w, be, jb, nu: (jb[w], 0, 0)),
                  smem(lambda w, be, jb, nu: (jb[jnp.minimum(w + 1, n_items - 1)], 0, 0)),
                  smem(lambda w, be, jb, nu: (w, 0, 0)),
                  smem(lambda w, be, jb, nu: (w + 1, 0, 0)),
                  pl.BlockSpec(memory_space=pl.ANY),
                  wspec((1, d, ff2)), wspec((1, 1, ff2)), wspec((1, ff, d)), wspec((1, 1, d))],
        out_specs=pl.BlockSpec(memory_space=pl.ANY),
        scratch_shapes=[pltpu.VMEM((MOE_ROWS * tr, 128), F32)] * 4
                       + [pltpu.SemaphoreType.DMA((2,)), pltpu.SemaphoreType.DMA],
    )
    return pl.pallas_call(
        _moe_body,
        grid_spec=grid_spec,
        out_shape=jax.ShapeDtypeStruct((out_rows * tr, 128), F32),
        compiler_params=_cparams(("arbitrary",)),
        name="moe",
    )(item_e, item_blk, n_used, src_blk, src_blk, dst_shift, dst_shift, h2t, w_up, b_up, w_down, b_down)


def _combine_body(o0_ref, o1_ref, o2_ref, o3_ref, tw_ref, x1_ref, mod_ref, g_ref, y_ref):
    tw = tw_ref[...]
    t = tw.shape[0]
    lane = lax.broadcasted_iota(jnp.int32, tw.shape, 1)
    acc = None
    for kk, o_ref in enumerate((o0_ref, o1_ref, o2_ref, o3_ref)):
        wk = jnp.sum(jnp.where(lane == kk, tw, 0.0), axis=-1, keepdims=True)
        terms = [p * wk for p in _load_token_tiles(o_ref, t)]
        acc = terms if acc is None else [a + b for a, b in zip(acc, terms)]
    d = x1_ref.shape[1]
    ms = sum(jnp.sum(a * a, axis=-1, keepdims=True) for a in acc) * (1.0 / d)
    inv = lax.rsqrt(ms + EPS)
    for s, a in enumerate(acc):
        ls = slice(s * 128, (s + 1) * 128)
        y_ref[:, ls] = x1_ref[:, ls] + mod_ref[0, 0, :, ls] * (a * inv * g_ref[:, ls])


def _combine(moe_out, tw, x1, mods, g_post_ffn, n_pad, row0, rows, tile, tiles_per_mod):
    d = x1.shape[1]
    r = mods.shape[2]
    assert row0 % tile == 0 and n_pad % tile == 0 and rows % tile == 0
    t0 = row0 // tile
    kt = n_pad // tile
    ospec = lambda kk: pl.BlockSpec((tile * TOKEN_TILE_ROWS, 128), lambda i: (kk * kt + t0 + i, 0))
    return pl.pallas_call(
        _combine_body,
        grid=(rows // tile,),
        in_specs=[ospec(0), ospec(1), ospec(2), ospec(3),
                  pl.BlockSpec((tile, 128), lambda i: (t0 + i, 0)),
                  pl.BlockSpec((tile, d), lambda i: (t0 + i, 0)),
                  pl.BlockSpec((1, 1, r, d), lambda i: (0, i // tiles_per_mod, 0, 0)),
                  pl.BlockSpec((1, d), lambda i: (0, 0))],
        out_specs=pl.BlockSpec((tile, d), lambda i: (i, 0)),
        out_shape=jax.ShapeDtypeStruct((rows, d), F32),
        compiler_params=_cparams(("parallel",)), name="combine",
    )(moe_out, moe_out, moe_out, moe_out, tw, x1, mods, g_post_ffn)


def _rope_table(pos):
    inv = 1.0 / (ROPE_THETA ** (jnp.arange(0, MLA_DROPE, 2, dtype=F32) / MLA_DROPE))
    ang = pos.astype(F32)[:, None] * inv[None, :]
    c, s = jnp.cos(ang), jnp.sin(ang)
    return jnp.concatenate([c, c, s, s], axis=-1)


def _rotate_cols(w):
    half = w.shape[-1] // 2
    return jnp.concatenate([-w[..., half:], w[..., :half]], axis=-1)


def _prep_weights(w_in, b_in, ml_f_bias, mla_g_q, w_uq, mla_g_kv, w_uk, w_uv, g_pre_mix):
    qk_w = ML_HEADS * ML_DQK
    v_w = ML_HEADS * ML_DV
    sizes = [qk_w, qk_w, v_w, ML_HEADS, ML_HEADS, v_w, MLA_Q_RANK, MLA_KV_RANK, MLA_DROPE]
    pts = [int(p) for p in np.cumsum(sizes)[:-1]]
    wq, wk, wv, wi, wf, wo, wql, wkv, wpe = jnp.split(w_in, pts, axis=-1)
    bq, bk, bv, bi, bf, bo, bql, bkv, bpe = jnp.split(b_in, pts, axis=-1)
    d = w_in.shape[0]
    pad_w = jnp.zeros((d, 128 - 2 * ML_HEADS), F32)
    pad_b = jnp.zeros((128 - 2 * ML_HEADS,), F32)
    w_small = jnp.concatenate([wql, wkv, wpe, _rotate_cols(wpe), wi, wf, pad_w], axis=-1)
    b_small = jnp.concatenate([bql, bkv, bpe, _rotate_cols(bpe), bi, bf, pad_b], axis=-1)
    f_bias = jnp.concatenate([jnp.zeros((ML_HEADS,), F32), ml_f_bias, pad_b], axis=-1)
    wuq_h = w_uq.reshape(MLA_Q_RANK, MLA_HEADS, MLA_DNOPE + MLA_DROPE)
    wuq_r = wuq_h[..., MLA_DNOPE:]
    wuq_ext = jnp.concatenate([wuq_h, _rotate_cols(wuq_r)], axis=-1).reshape(MLA_Q_RANK, MLA_HEADS * QK_PAD)
    return {
        "g_pre_mix": g_pre_mix[None, :],
        "w_main": jnp.concatenate([wq, wk, wv, wo], axis=-1).astype(BF16),
        "b_main": jnp.concatenate([bq, bk, bv, bo], axis=-1)[None, :],
        "w_small": w_small.astype(BF16), "b_small": b_small[None, :], "f_bias": f_bias[None, :],
        "g_q": mla_g_q[None, :], "g_kv": mla_g_kv[None, :],
        "w_uq": wuq_ext.astype(BF16), "w_uk": w_uk.astype(BF16), "w_uv": w_uv.astype(BF16),
        "w_uv_t": w_uv.reshape(MLA_KV_RANK, MLA_HEADS, MLA_DV).transpose(1, 2, 0).astype(BF16),
    }


def _route_items(te, n_tok, n_pad):
    n_assign = n_tok * TOP_K
    n_blocks = -(-n_assign // MOE_ROWS)
    n_items = n_blocks + N_EXPERTS - 1
    i32 = jnp.int32
    e_flat = te.reshape(-1)
    skey = jnp.sort(e_flat * (1 << ORDER_BITS) + jnp.arange(n_assign, dtype=i32))
    order = skey & ((1 << ORDER_BITS) - 1)
    start = jnp.sum(e_flat[None, :] < jnp.arange(N_EXPERTS + 1, dtype=i32)[:, None], axis=1).astype(i32)
    lo_e, hi_e = start[:-1], start[1:]
    first_blk = lo_e // MOE_ROWS
    n_it = jnp.where(hi_e > lo_e, (hi_e - 1) // MOE_ROWS - first_blk + 1, 0)
    it_end = jnp.cumsum(n_it)
    n_used = it_end[-1]
    w = jnp.minimum(jnp.arange(n_items, dtype=i32), n_used - 1)
    item_e = jnp.minimum(jnp.searchsorted(it_end, w, side="right", method="compare_all"), N_EXPERTS - 1).astype(i32)
    item_blk = first_blk[item_e] + w - (it_end - n_it)[item_e]
    r = jnp.arange(MOE_ROWS, dtype=i32)[None, :]
    pos = item_blk[:, None] * MOE_ROWS + r
    mine = jnp.logical_and(pos >= lo_e[item_e][:, None], pos < hi_e[item_e][:, None])
    order_p = jnp.pad(order, (0, n_blocks * MOE_ROWS - n_assign)).reshape(n_blocks, MOE_ROWS)
    src_blk = order_p >> 2
    dst_blk = (order_p & 3) * n_pad + src_blk
    spare = TOP_K * n_pad + r
    dst_item = jnp.where(mine, dst_blk[item_blk], spare)
    dst_shift = jnp.concatenate([spare, dst_item], axis=0)
    tr = TOKEN_TILE_ROWS
    return item_e, item_blk, n_used.reshape(1).astype(i32), (src_blk * tr)[:, None, :], (dst_shift * tr)[:, None, :]


def kernel(x_prompt, x_sample, cache_kv_latent, cache_k_rope, state_mlstm_C, state_mlstm_n, state_mlstm_m,
           c_prompt, c_sample, w_ada, b_ada, g_pre_mix, g_post_mix, g_pre_ffn, g_post_ffn, w_in, b_in,
           ml_f_bias, ml_head_g, mla_g_q, w_uq, mla_g_kv, w_uk, w_uv, w_br_ml, w_br_mla, w_gate, b_gate,
           w_out, w_router, b_router, w_up, b_up, w_down, b_down):
    assert w_ada.shape[0] == 1, "single-layer step"
    bp, sp, d = x_prompt.shape
    bs, ss, _ = x_sample.shape
    past = cache_kv_latent.shape[2]
    n_p, n_s = bp * sp, bs * ss
    n_tok = n_p + n_s
    tile_p = TOK_TILE
    tile_s = min(TOK_TILE, n_s)
    n_pad = -(-n_tok // tile_p) * tile_p
    assert n_tok * TOP_K < (1 << ORDER_BITS) and d == TOKEN_TILE_ROWS * 128
    assert sp % ATT_TILE == 0 and sp % ML_CHUNK == 0 and sp % tile_p == 0
    assert n_s % tile_s == 0 and n_p % tile_s == 0 and n_pad % tile_s == 0

    mods = _ada(jnp.concatenate([c_prompt, c_sample], axis=0), w_ada[0], b_ada[0][None, :])
    mods = mods.reshape(bp + bs, 6, d).transpose(1, 0, 2)
    mods_p = mods[:, :bp][:, :, None, :]
    mods_s = jnp.broadcast_to(mods[:, bp:, None, :], (6, bs, ss, d)).reshape(6, n_s // tile_s, tile_s, d)

    wm = _prep_weights(w_in[0], b_in[0], ml_f_bias[0], mla_g_q[0], w_uq[0], mla_g_kv[0], w_uk[0], w_uv[0],
                       g_pre_mix[0])
    cs_p = _rope_table(jnp.arange(sp))
    cs_s = jnp.tile(_rope_table(past + jnp.arange(ss)), (bs, 1))

    xp = x_prompt.reshape(n_p, d)
    xs = x_sample.reshape(n_s, d)
    tiles_per_stream = sp // tile_p
    mp = _mixer_in(xp, mods_p[:2], cs_p, wm, tile_p, tiles_per_stream, tiles_per_stream)
    ms = _mixer_in(xs, mods_s[:2], cs_s, wm, tile_s, 1, n_s // tile_s)

    g_head = ml_head_g[0][None, :]

    def gate_rows(g, b, s):
        return g[:, :2 * ML_HEADS].reshape(b, s, 2 * ML_HEADS).transpose(0, 2, 1)

    qp, kp, vp, op, gp, lat_p, kr_p, qcat_p, kcat_p, vv_p = mp
    zc = jnp.zeros((bp, ML_HEADS, ML_DQK, ML_DV), F32)
    zn = jnp.zeros((bp, ML_HEADS, 1, ML_DQK), F32)
    zm = jnp.zeros((bp, ML_HEADS, 1, 1), F32)
    yml_p, c_p, nn_p, m_p = _mlstm(qp, kp, vp, op, gp, gate_rows(gp, bp, sp), zc, zn, zm, g_head, ML_CHUNK)
    ymla_p = _flash(qcat_p, kcat_p, vv_p, bp, sp, ATT_TILE)

    qs, ks, vs, os_, gs, lat_s, kr_s, qcat_s, kcat_s, _ = ms
    yml_s, c_s, nn_s, m_s = _mlstm(qs, ks, vs, os_, gs, gate_rows(gs, bs, ss),
                                   state_mlstm_C[0], state_mlstm_n[0][:, :, None, :],
                                   state_mlstm_m[0][:, :, None, None], g_head, ss)
    ckr_pad = jnp.pad(cache_k_rope[0], ((0, 0), (0, 0), (0, 128 - MLA_DROPE)))
    ymla_s = _sattn(qcat_s, cache_kv_latent[0], lat_s, ckr_pad, kcat_s, wm["w_uk"], wm["w_uv"], ss)

    w_r = jnp.pad(w_router[0], ((0, 0), (0, 128 - N_EXPERTS)))
    w_r_hi = w_r.astype(BF16)
    wt = {
        "g_pre_mix": g_pre_mix[0][None, :], "g_post_mix": g_post_mix[0][None, :], "g_pre_ffn": g_pre_ffn[0][None, :],
        "w_gate": w_gate[0].astype(BF16), "b_gate": b_gate[0][None, :],
        "w_br_ml": w_br_ml[0].astype(BF16), "w_br_mla": w_br_mla[0].astype(BF16), "w_out": w_out[0].astype(BF16),
        "w_router_hi": w_r_hi, "w_router_lo": (w_r - w_r_hi.astype(F32)).astype(BF16),
        "b_router": jnp.pad(b_router[0], (0, 128 - N_EXPERTS))[None, :],
    }
    merged = _tail(xp, yml_p, ymla_p, mods_p[:5], wt, tile_p, tiles_per_stream, n_tok)
    x1, h2, te, tw = _tail(xs, yml_s, ymla_s, mods_s[:5], wt, tile_s, 1, n_tok, row0=n_p, into=merged)
    te = te[:, :TOP_K]

    item_e, item_blk, n_used, src_blk, dst_shift = _route_items(te, n_tok, n_pad)
    moe_out = _moe(item_e, item_blk, n_used, src_blk, dst_shift, h2, w_up[0].astype(BF16), b_up[0][:, None, :],
                   w_down[0].astype(BF16), b_down[0][:, None, :], n_pad * TOP_K + MOE_ROWS)
    gff = g_post_ffn[0][None, :]
    y_p = _combine(moe_out, tw, x1, mods_p[5:], gff, n_pad, 0, n_p, tile_p, tiles_per_stream)
    y_s = _combine(moe_out, tw, x1, mods_s[5:], gff, n_pad, n_p, n_s, tile_s, 1)

    return (y_p.reshape(bp, sp, d), y_s.reshape(bs, ss, d),
            lat_p.reshape(1, bp, sp, MLA_KV_RANK), kr_p.reshape(1, bp, sp, MLA_DROPE),
            c_p[None], nn_p.reshape(1, bp, ML_HEADS, ML_DQK), m_p.reshape(1, bp, ML_HEADS),
            lat_s.reshape(1, bs, ss, MLA_KV_RANK), kr_s.reshape(1, bs, ss, MLA_DROPE),
            c_s[None], nn_s.reshape(1, bs, ML_HEADS, ML_DQK), m_s.reshape(1, bs, ML_HEADS))
```

```python
import functools

import jax
import jax.numpy as jnp
import numpy as np
from jax import lax
from jax.experimental import pallas as pl
from jax.experimental.pallas import tpu as pltpu

F32 = jnp.float32
BF16 = jnp.bfloat16

EPS = 1e-6
ROPE_THETA = 10000.0
SWIGLU_ALPHA = 1.702
SWIGLU_LIMIT = 7.0
TOP_K = 4

ML_HEADS = 4
ML_DQK = 128
ML_DV = 256
MLA_HEADS = 8
MLA_DNOPE = 128
MLA_DROPE = 64
MLA_DV = 128
MLA_Q_RANK = 384
MLA_KV_RANK = 256
MLA_SCALE = (MLA_DNOPE + MLA_DROPE) ** -0.5
Q_SCALE = MLA_SCALE * float(np.log2(np.e))
QK_PAD = 256
N_EXPERTS = 32

VMEM_LIMIT_BYTES = 52 * 1024 * 1024

TOK_TILE = 512
ML_CHUNK = 256
ATT_TILE = 1024
FLASH_DIAG_CHUNK = 512
FLASH_UNROLL = 2
STREAM_CHUNK = 64
MOE_ROWS = 512
TOKEN_TILE_ROWS = 8
ORDER_BITS = 18


def _cparams(sem, vmem=VMEM_LIMIT_BYTES):
    return pltpu.CompilerParams(dimension_semantics=sem, vmem_limit_bytes=vmem)


def _resident(a):
    return pl.BlockSpec(a.shape, lambda i: (0,) * a.ndim, pipeline_mode=pl.Buffered(1))


def _dot(a, b):
    return jnp.dot(a, b, preferred_element_type=F32)


def _dot_nt(a, b):
    return lax.dot_general(a, b, (((1,), (1,)), ((), ())), preferred_element_type=F32)


def _rms(x, g):
    return x * lax.rsqrt(jnp.mean(x * x, axis=-1, keepdims=True) + EPS) * g


def _store_token_tiles(ref, x):
    t = x.shape[0]
    for s in range(TOKEN_TILE_ROWS):
        ref[pl.ds(s, t, stride=TOKEN_TILE_ROWS), :] = x[:, s * 128:(s + 1) * 128]


def _load_token_tiles(ref, t):
    return [ref[pl.ds(s, t, stride=TOKEN_TILE_ROWS), :] for s in range(TOKEN_TILE_ROWS)]


def _split3(x):
    hi = x.astype(BF16)
    r1 = x - hi.astype(F32)
    mid = r1.astype(BF16)
    lo = (r1 - mid.astype(F32)).astype(BF16)
    return hi, mid, lo


def _ada_body(c_ref, w_ref, b_ref, o_ref):
    o_ref[...] = _dot(c_ref[...].astype(BF16), w_ref[...].astype(BF16)) + b_ref[...]


def _ada(c, w, b):
    m, d = c.shape
    n = w.shape[1]
    return pl.pallas_call(
        _ada_body,
        grid=(n // d,),
        in_specs=[pl.BlockSpec((m, d), lambda j: (0, 0)),
                  pl.BlockSpec((d, d), lambda j: (0, j)),
                  pl.BlockSpec((1, d), lambda j: (0, j))],
        out_specs=pl.BlockSpec((m, d), lambda j: (0, j)),
        out_shape=jax.ShapeDtypeStruct((m, n), F32),
        compiler_params=_cparams(("parallel",)), name="ada",
    )(c, w, b)


_SM_QLAT = 0
_SM_KVLAT = MLA_Q_RANK
_SM_KPE = MLA_Q_RANK + MLA_KV_RANK
_SM_GATE = _SM_KPE + 128
_SM_W = _SM_GATE + 128


def _mixer_in_body(x_ref, mod_ref, gpre_ref, wm_ref, bm_ref, ws_ref, bs_ref, fb_ref, gq_ref, gkv_ref,
                   wuq_ref, wuk_ref, wuvt_ref, cs_ref,
                   q_ref, k_ref, v_ref, o_ref, gate_ref, lat_ref, kr_ref, qcat_ref, kcat_ref, vvt_ref):
    x = x_ref[...]
    shift = mod_ref[0, 0]
    scale = mod_ref[1, 0]
    h = _rms(x, gpre_ref[...]) * (1.0 + scale) + shift
    hb = h.astype(BF16)
    qk_w = ML_HEADS * ML_DQK
    v_w = ML_HEADS * ML_DV
    q_ref[...] = (_dot(hb, wm_ref[:, 0:qk_w]) + bm_ref[:, 0:qk_w]).astype(BF16)
    k_ref[...] = ((_dot(hb, wm_ref[:, qk_w:2 * qk_w]) + bm_ref[:, qk_w:2 * qk_w]) * (ML_DQK ** -0.5)).astype(BF16)
    v_ref[...] = (_dot(hb, wm_ref[:, 2 * qk_w:2 * qk_w + v_w]) + bm_ref[:, 2 * qk_w:2 * qk_w + v_w]).astype(BF16)
    o_ref[...] = (_dot(hb, wm_ref[:, 2 * qk_w + v_w:]) + bm_ref[:, 2 * qk_w + v_w:]).astype(BF16)

    zs = _dot(hb, ws_ref[...]) + bs_ref[...]
    g = zs[:, _SM_GATE:_SM_GATE + 128]
    gf = g + fb_ref[...]
    logsig = jnp.minimum(gf, 0.0) - jnp.log1p(jnp.exp(-jnp.abs(gf)))
    lane = lax.broadcasted_iota(jnp.int32, g.shape, 1)
    gate_ref[...] = jnp.where(lane >= ML_HEADS, logsig, g)

    cs = cs_ref[...]
    qln = _rms(zs[:, _SM_QLAT:_SM_QLAT + MLA_Q_RANK], gq_ref[...]).astype(BF16)
    cq = _dot(qln, wuq_ref[...])
    for hd in range(MLA_HEADS):
        blk = cq[:, hd * QK_PAD:(hd + 1) * QK_PAD]
        t = blk[:, MLA_DNOPE:] * cs
        rot = t + pltpu.roll(t, 64, 1)
        qcat_ref[hd, :, 0:MLA_DNOPE] = (blk[:, :MLA_DNOPE] * Q_SCALE).astype(BF16)
        qcat_ref[hd, :, MLA_DNOPE:] = (rot * Q_SCALE).astype(BF16)
    latent = _rms(zs[:, _SM_KVLAT:_SM_KVLAT + MLA_KV_RANK], gkv_ref[...])
    lat_ref[...] = latent
    tk = zs[:, _SM_KPE:_SM_KPE + 128] * cs
    rotk = tk + pltpu.roll(tk, 64, 1)
    kr_ref[...] = rotk[:, 0:MLA_DROPE]
    krz = jnp.where(lane < MLA_DROPE, rotk, 0.0).astype(BF16)
    latb = latent.astype(BF16)
    kn = _dot(latb, wuk_ref[...])
    for hd in range(MLA_HEADS):
        kcat_ref[hd, :, 0:MLA_DNOPE] = kn[:, hd * MLA_DNOPE:(hd + 1) * MLA_DNOPE].astype(BF16)
        kcat_ref[hd, :, MLA_DNOPE:] = krz
        vvt_ref[hd] = _dot_nt(wuvt_ref[hd], latb).astype(BF16)


def _mixer_in(x, mods, cs, w, tile, tiles_per_mod, cs_tiles):
    n, d = x.shape
    nm, _, r, _ = mods.shape
    grid = (n // tile,)
    const = _resident
    tok = lambda width: pl.BlockSpec((tile, width), lambda i: (i, 0))
    head = lambda width: pl.BlockSpec((MLA_HEADS, tile, width), lambda i: (0, i, 0))
    qk_w = ML_HEADS * ML_DQK
    v_w = ML_HEADS * ML_DV
    out_shape = (
        jax.ShapeDtypeStruct((n, qk_w), BF16), jax.ShapeDtypeStruct((n, qk_w), BF16),
        jax.ShapeDtypeStruct((n, v_w), BF16), jax.ShapeDtypeStruct((n, v_w), BF16),
        jax.ShapeDtypeStruct((n, 128), F32),
        jax.ShapeDtypeStruct((n, MLA_KV_RANK), F32), jax.ShapeDtypeStruct((n, MLA_DROPE), F32),
        jax.ShapeDtypeStruct((MLA_HEADS, n, QK_PAD), BF16), jax.ShapeDtypeStruct((MLA_HEADS, n, QK_PAD), BF16),
        jax.ShapeDtypeStruct((MLA_HEADS, MLA_DV, n), BF16),
    )
    out_specs = (tok(qk_w), tok(qk_w), tok(v_w), tok(v_w), tok(128), tok(MLA_KV_RANK), tok(MLA_DROPE),
                 head(QK_PAD), head(QK_PAD), pl.BlockSpec((MLA_HEADS, MLA_DV, tile), lambda i: (0, 0, i)))
    in_specs = [
        tok(d),
        pl.BlockSpec((nm, 1, r, d), lambda i: (0, i // tiles_per_mod, 0, 0)),
        const(w["g_pre_mix"]), const(w["w_main"]), const(w["b_main"]), const(w["w_small"]), const(w["b_small"]),
        const(w["f_bias"]), const(w["g_q"]), const(w["g_kv"]), const(w["w_uq"]), const(w["w_uk"]), const(w["w_uv_t"]),
        pl.BlockSpec((tile, 128), lambda i: (i % cs_tiles, 0)),
    ]
    return pl.pallas_call(
        _mixer_in_body, grid=grid, in_specs=in_specs, out_specs=out_specs, out_shape=out_shape,
        compiler_params=_cparams(("parallel",)), name="mixer_in",
    )(x, mods, w["g_pre_mix"], w["w_main"], w["b_main"], w["w_small"], w["b_small"], w["f_bias"],
      w["g_q"], w["g_kv"], w["w_uq"], w["w_uk"], w["w_uv_t"], cs)


def _mlstm_body(q_ref, k_ref, v_ref, o_ref, gc_ref, gr_ref, c0_ref, n0_ref, m0_ref, gh_ref,
                y_ref, c_ref, n_ref, m_ref, br_ref):
    chunk = q_ref.shape[0]

    @pl.when(pl.program_id(1) == 0)
    def _():
        c_ref[...] = c0_ref[...]
        n_ref[...] = n0_ref[...]
        m_ref[...] = m0_ref[...]

    row = lax.broadcasted_iota(jnp.int32, (chunk, chunk), 0)
    col = lax.broadcasted_iota(jnp.int32, (chunk, chunk), 1)
    causal = col <= row
    tri = causal.astype(BF16)
    tri_t = (row <= col).astype(BF16)

    gc = gc_ref[...]
    gr = gr_ref[0]
    bc_all = sum(_dot(tri, p) for p in _split3(gc))
    br_ref[...] = sum(_dot(p, tri_t) for p in _split3(gr))

    lane = lax.broadcasted_iota(jnp.int32, gc.shape, 1)
    rowv = lax.broadcasted_iota(jnp.int32, (chunk, 1), 0)
    is_last = rowv == chunk - 1

    def pick(a, j):
        return jnp.sum(jnp.where(lane == j, a, 0.0), axis=-1, keepdims=True)

    def last(a):
        return jnp.sum(jnp.where(is_last, a, 0.0), axis=0, keepdims=True)

    for hd in range(ML_HEADS):
        qs = slice(hd * ML_DQK, (hd + 1) * ML_DQK)
        vs = slice(hd * ML_DV, (hd + 1) * ML_DV)
        q = q_ref[:, qs]
        k = k_ref[:, qs]
        v = v_ref[:, vs]
        ig_c = pick(gc, hd)
        b_c = pick(bc_all, ML_HEADS + hd)
        ig_r = gr_ref[0, hd:hd + 1, :]
        b_r = br_ref[ML_HEADS + hd:ML_HEADS + hd + 1, :]
        m_prev = m_ref[0, hd]
        c_prev = c_ref[0, hd]
        n_prev = n_ref[0, hd]

        dmat = jnp.where(causal, b_c - b_r + ig_r, -jnp.inf)
        inter = b_c + m_prev
        m_t = jnp.maximum(inter, jnp.max(dmat, axis=-1, keepdims=True))
        w_intra = jnp.exp(dmat - m_t)
        w_inter = jnp.exp(inter - m_t)
        qk = _dot_nt(q, k) * w_intra
        num = w_inter * _dot(q, c_prev.astype(BF16)) + _dot(qk.astype(BF16), v)
        qn = jnp.sum(q.astype(F32) * n_prev, axis=-1, keepdims=True)
        den = w_inter * qn + jnp.sum(qk, axis=-1, keepdims=True)
        hcur = num / jnp.maximum(jnp.abs(den), jnp.exp(-m_t))
        hn = hcur * lax.rsqrt(jnp.mean(hcur * hcur, axis=-1, keepdims=True) + EPS)
        y_ref[:, vs] = (hn * gh_ref[:, vs] * jax.nn.sigmoid(o_ref[:, vs].astype(F32))).astype(BF16)

        m_new = last(m_t)
        w_prev = jnp.exp(last(inter) - m_new)
        w_end = jnp.exp(last(b_c) - b_c + ig_c - m_new)
        kw = k.astype(F32) * w_end
        c_ref[0, hd] = w_prev * c_prev + _dot(kw.T.astype(BF16), v)
        n_ref[0, hd] = w_prev * n_prev + jnp.sum(kw, axis=0, keepdims=True)
        m_ref[0, hd] = m_new


def _mlstm(q, k, v, o, gates_col, gates_row, c0, n0, m0, g_head, chunk):
    n = q.shape[0]
    b, _, s = gates_row.shape
    nc = s // chunk
    tok = lambda width: pl.BlockSpec((chunk, width), lambda i, c: (i * nc + c, 0))
    state = lambda a: pl.BlockSpec((1,) + a.shape[1:], lambda i, c: (i,) + (0,) * (a.ndim - 1))
    qk_w = ML_HEADS * ML_DQK
    v_w = ML_HEADS * ML_DV
    return pl.pallas_call(
        _mlstm_body,
        grid=(b, nc),
        in_specs=[tok(qk_w), tok(qk_w), tok(v_w), tok(v_w), tok(128),
                  pl.BlockSpec((1, 8, chunk), lambda i, c: (i, 0, c)),
                  state(c0), state(n0), state(m0),
                  pl.BlockSpec((1, v_w), lambda i, c: (0, 0))],
        out_specs=(tok(v_w), state(c0), state(n0), state(m0)),
        out_shape=(jax.ShapeDtypeStruct((n, v_w), BF16),
                   jax.ShapeDtypeStruct(c0.shape, F32), jax.ShapeDtypeStruct(n0.shape, F32),
                   jax.ShapeDtypeStruct(m0.shape, F32)),
        scratch_shapes=[pltpu.VMEM((8, chunk), F32)],
        compiler_params=_cparams(("parallel", "arbitrary")), name="mlstm",
    )(q, k, v, o, gates_col, gates_row, c0, n0, m0, g_head)


def _flash_body(q_ref, k_ref, vt_ref, o_ref, m_sc, l_sc, acc_sc):
    tile = q_ref.shape[1]
    qi = pl.program_id(2)
    m_sc[...] = jnp.full(m_sc.shape, -jnp.inf, F32)
    l_sc[...] = jnp.zeros(l_sc.shape, F32)
    acc_sc[...] = jnp.zeros(acc_sc.shape, F32)

    def update(koff, ksize, q0, masked):
        qs = slice(q0, tile)
        kt = k_ref[0, pl.ds(koff, ksize), :]
        vt = vt_ref[0, :, pl.ds(koff, ksize)]
        st = _dot_nt(kt, q_ref[0, qs, :])
        if masked:
            shift = STREAM_CHUNK.bit_length() - 1
            kc = lax.broadcasted_iota(jnp.int32, st.shape, 0) >> shift
            qc = lax.broadcasted_iota(jnp.int32, st.shape, 1) >> shift
            st = jnp.where(kc <= qc, st, -jnp.inf)
        m_old = m_sc[:, qs]
        m_new = jnp.maximum(m_old, jnp.max(st, axis=0, keepdims=True))
        alpha = jnp.exp2(m_old - m_new)
        p = jnp.exp2(st - m_new)
        l_sc[:, qs] = alpha * l_sc[:, qs] + jnp.sum(p, axis=0, keepdims=True)
        acc_sc[:, qs] = alpha * acc_sc[:, qs] + _dot(vt, p.astype(BF16))
        m_sc[:, qs] = m_new

    def step(j, masked):
        off = pl.multiple_of(j * tile, tile)
        if not masked:
            update(off, tile, 0, False)
        else:
            for k0 in range(0, tile, FLASH_DIAG_CHUNK):
                update(off + k0, FLASH_DIAG_CHUNK, k0, True)

    def body(i, carry):
        for u in range(FLASH_UNROLL):
            step(FLASH_UNROLL * i + u, False)
        return carry

    def rest(j, carry):
        step(j, False)
        return carry

    n_full = qi // FLASH_UNROLL
    lax.fori_loop(0, n_full, body, 0)
    lax.fori_loop(n_full * FLASH_UNROLL, qi, rest, 0)
    step(qi, True)
    o_ref[...] = (acc_sc[...] / l_sc[...]).T.astype(BF16)


def _flash(qcat, kcat, vvt, batch, seq, tile):
    heads, n, _ = qcat.shape
    nq = seq // tile
    return pl.pallas_call(
        _flash_body,
        grid=(batch, heads, nq),
        in_specs=[pl.BlockSpec((1, tile, QK_PAD), lambda b, h, i: (h, b * nq + i, 0)),
                  pl.BlockSpec((1, seq, QK_PAD), lambda b, h, i: (h, b, 0)),
                  pl.BlockSpec((1, MLA_DV, seq), lambda b, h, i: (h, 0, b))],
        out_specs=pl.BlockSpec((tile, MLA_DV), lambda b, h, i: (b * nq + i, h)),
        out_shape=jax.ShapeDtypeStruct((n, heads * MLA_DV), BF16),
        scratch_shapes=[pltpu.VMEM((1, tile), F32), pltpu.VMEM((1, tile), F32), pltpu.VMEM((MLA_DV, tile), F32)],
        compiler_params=_cparams(("parallel", "parallel", "arbitrary")),
        name="flash",
    )(qcat, kcat, vvt)


def _sattn_body(q_ref, clat_ref, nlat_ref, ckr_ref, nk_ref, wuk_ref, wuv_ref, y_ref, qa_sc, qr_sc):
    s_new = nlat_ref.shape[0]
    for hd in range(MLA_HEADS):
        qh = q_ref[hd]
        hs = slice(hd * MLA_DNOPE, (hd + 1) * MLA_DNOPE)
        qa_sc[hd * s_new:(hd + 1) * s_new, :] = _dot_nt(qh[:, 0:MLA_DNOPE], wuk_ref[:, hs]).astype(BF16)
        qr_sc[hd * s_new:(hd + 1) * s_new, :] = qh[:, MLA_DNOPE:]
    qa = qa_sc[...]
    qr = qr_sc[...]
    clat = clat_ref[0].astype(BF16)
    nlat = nlat_ref[...].astype(BF16)
    ckr = ckr_ref[0].astype(BF16)
    nkr = nk_ref[0, :, MLA_DNOPE:]
    s_c = _dot_nt(qa, clat) + _dot_nt(qr, ckr)
    s_n = _dot_nt(qa, nlat) + _dot_nt(qr, nkr)
    m = jnp.maximum(jnp.max(s_c, axis=-1, keepdims=True), jnp.max(s_n, axis=-1, keepdims=True))
    p_c = jnp.exp2(s_c - m)
    p_n = jnp.exp2(s_n - m)
    l = jnp.sum(p_c, axis=-1, keepdims=True) + jnp.sum(p_n, axis=-1, keepdims=True)
    o_lat = ((_dot(p_c.astype(BF16), clat) + _dot(p_n.astype(BF16), nlat)) / l).astype(BF16)
    for hd in range(MLA_HEADS):
        hs = slice(hd * MLA_DV, (hd + 1) * MLA_DV)
        y_ref[:, hs] = _dot(o_lat[hd * s_new:(hd + 1) * s_new, :], wuv_ref[:, hs]).astype(BF16)


def _sattn(qcat, cache_lat, new_lat, cache_kr_pad, kcat, w_uk, w_uv, s_new):
    heads, n, _ = qcat.shape
    b, past, _ = cache_lat.shape
    const = _resident
    return pl.pallas_call(
        _sattn_body,
        grid=(b,),
        in_specs=[pl.BlockSpec((heads, s_new, QK_PAD), lambda i: (0, i, 0)),
                  pl.BlockSpec((1, past, MLA_KV_RANK), lambda i: (i, 0, 0)),
                  pl.BlockSpec((s_new, MLA_KV_RANK), lambda i: (i, 0)),
                  pl.BlockSpec((1, past, 128), lambda i: (i, 0, 0)),
                  pl.BlockSpec((1, s_new, QK_PAD), lambda i: (0, i, 0)),
                  const(w_uk), const(w_uv)],
        out_specs=pl.BlockSpec((s_new, heads * MLA_DV), lambda i: (i, 0)),
        out_shape=jax.ShapeDtypeStruct((n, heads * MLA_DV), BF16),
        scratch_shapes=[pltpu.VMEM((heads * s_new, MLA_KV_RANK), BF16), pltpu.VMEM((heads * s_new, 128), BF16)],
        compiler_params=_cparams(("parallel",)), name="sattn",
    )(qcat, cache_lat, new_lat, cache_kr_pad, kcat, w_uk, w_uv)


def _tail_body(x_ref, yml_ref, ymla_ref, mod_ref, gpre_ref, gpost_ref, gffn_ref, wg_ref, bg_ref,
               wml_ref, wmla_ref, wout_ref, wrh_ref, wrl_ref, br_ref,
               x1_ref, h2_ref, te_ref, tw_ref):
    x = x_ref[...]
    d = x.shape[1]
    sh1, sc1, gt1, sh2, sc2 = (mod_ref[j, 0] for j in range(5))
    hb = (_rms(x, gpre_ref[...]) * (1.0 + sc1) + sh1).astype(BF16)
    g = jax.nn.sigmoid(_dot(hb, wg_ref[...]) + bg_ref[...])
    yb = g[:, :d] * _dot(yml_ref[...], wml_ref[...]) + g[:, d:] * _dot(ymla_ref[...], wmla_ref[...])
    y = _dot(yb.astype(BF16), wout_ref[...])
    x1 = x + gt1 * _rms(y, gpost_ref[...])
    x1_ref[...] = x1
    h2 = _rms(x1, gffn_ref[...]) * (1.0 + sc2) + sh2
    _store_token_tiles(h2_ref, h2)
    hi = h2.astype(BF16)
    lo = (h2 - hi.astype(F32)).astype(BF16)
    logits = _dot(hi, wrh_ref[...]) + _dot(hi, wrl_ref[...]) + _dot(lo, wrh_ref[...]) + br_ref[...]
    lane = lax.broadcasted_iota(jnp.int32, logits.shape, 1)
    lane_f = lane.astype(F32)
    lg = jnp.where(lane < N_EXPERTS, logits, -jnp.inf)
    te = jnp.zeros(lg.shape, F32)
    tw = jnp.zeros(lg.shape, F32)
    top0 = None
    for kk in range(TOP_K):
        mx = jnp.max(lg, axis=-1, keepdims=True)
        idx = jnp.min(jnp.where(lg == mx, lane_f, 128.0), axis=-1, keepdims=True)
        if kk == 0:
            top0 = mx
        te = jnp.where(lane == kk, idx, te)
        tw = jnp.where(lane == kk, jnp.exp(mx - top0), tw)
        lg = jnp.where(lane_f == idx, -jnp.inf, lg)
    te_ref[...] = te.astype(jnp.int32)
    tw_ref[...] = tw / jnp.sum(tw, axis=-1, keepdims=True)


def _tail(x, yml, ymla, mods, w, tile, tiles_per_mod, n_total, row0=0, into=None):
    n, d = x.shape
    nm, _, r, _ = mods.shape
    assert nm == 5 and row0 % tile == 0
    t0 = row0 // tile
    const = _resident
    tok = lambda width: pl.BlockSpec((tile, width), lambda i: (i, 0))
    out = lambda rows, width: pl.BlockSpec((rows, width), lambda i: (t0 + i, 0))
    names = ("g_pre_mix", "g_post_mix", "g_pre_ffn", "w_gate", "b_gate", "w_br_ml", "w_br_mla", "w_out",
             "w_router_hi", "w_router_lo", "b_router")
    n_in = 4 + len(names)
    extra = () if into is None else tuple(into)

    def body(*refs):
        _tail_body(*refs[:n_in], *refs[n_in + len(extra):])

    return pl.pallas_call(
        body,
        grid=(n // tile,),
        in_specs=[tok(d), tok(d), tok(d),
                  pl.BlockSpec((5, 1, r, d), lambda i: (0, i // tiles_per_mod, 0, 0))]
                 + [const(w[k]) for k in names] + [pl.BlockSpec(memory_space=pl.ANY)] * len(extra),
        out_specs=(out(tile, d), out(tile * TOKEN_TILE_ROWS, 128), out(tile, 128), out(tile, 128)),
        out_shape=(jax.ShapeDtypeStruct((n_total, d), F32),
                   jax.ShapeDtypeStruct((n_total * TOKEN_TILE_ROWS, 128), F32),
                   jax.ShapeDtypeStruct((n_total, 128), jnp.int32), jax.ShapeDtypeStruct((n_total, 128), F32)),
        input_output_aliases={n_in + j: j for j in range(len(extra))},
        compiler_params=_cparams(("parallel",)), name="tail",
    )(x, yml, ymla, mods, *[w[k] for k in names], *extra)


def _moe_body(be_ref, jb_ref, nu_ref, src_ref, srcn_ref, dstp_ref, dst_ref, h2_hbm, wup_ref, bup_ref, wdn_ref,
              bdn_ref, out_hbm, xbuf0, xbuf1, obuf0, obuf1, sem_in, sem_out):
    del be_ref, jb_ref
    w = pl.program_id(0)
    n_used = nu_ref[0]
    rows = src_ref.shape[2]
    tr = TOKEN_TILE_ROWS
    ff = wdn_ref.shape[1]

    xbufs, obufs = (xbuf0, xbuf1), (obuf0, obuf1)

    def gather_copy(idx_ref, slot, i):
        return pltpu.make_async_copy(h2_hbm.at[pl.ds(pl.multiple_of(idx_ref[0, 0, i], tr), tr)],
                                     xbufs[slot].at[pl.ds(i * tr, tr)], sem_in.at[slot])

    def scatter_copy(idx_ref, slot, i):
        return pltpu.make_async_copy(obufs[slot].at[pl.ds(i * tr, tr)],
                                     out_hbm.at[pl.ds(pl.multiple_of(idx_ref[0, 0, i], tr), tr)], sem_out)

    def gather_wait(slot):
        pltpu.make_async_copy(h2_hbm.at[pl.ds(0, rows * tr)], xbufs[slot], sem_in.at[slot]).wait()

    def scatter_wait(slot):
        pltpu.make_async_copy(obufs[slot], out_hbm.at[pl.ds(0, rows * tr)], sem_out).wait()

    def rolled(copy, idx_ref, slot):
        def issue(i, c):
            copy(idx_ref, slot, i).start()
            return c
        lax.fori_loop(0, rows, issue, 0, unroll=8)

    def item(slot):
        other = 1 - slot
        @pl.when(w == 0)
        def _():
            rolled(gather_copy, src_ref, slot)
            obufs[other][...] = jnp.zeros(obufs[other].shape, F32)

        gather_wait(slot)

        @pl.when(w > 0)
        def _():
            scatter_wait(slot)

        x = jnp.concatenate(_load_token_tiles(xbufs[slot], rows), axis=-1).astype(BF16)
        for i in range(rows):
            gather_copy(srcn_ref, other, i).start()
            scatter_copy(dstp_ref, other, i).start()
        gu = _dot(x, wup_ref[0]) + bup_ref[0]
        gl = jnp.minimum(gu[:, :ff], SWIGLU_LIMIT)
        up = jnp.clip(gu[:, ff:], -SWIGLU_LIMIT, SWIGLU_LIMIT)
        act = (up + 1.0) * gl * jax.nn.sigmoid(SWIGLU_ALPHA * gl)
        _store_token_tiles(obufs[slot], _dot(act.astype(BF16), wdn_ref[0]) + bdn_ref[0])

        @pl.when(w == n_used - 1)
        def _():
            gather_wait(other)
            scatter_wait(other)
            rolled(scatter_copy, dst_ref, slot)
            scatter_wait(slot)

    for slot in (1, 0):
        pl.when(jnp.logical_and(w < n_used, (w & 1) == slot))(functools.partial(item, slot))


def _moe(item_e, item_blk, n_used, src_blk, dst_shift, h2t, w_up, b_up, w_down, b_down, out_rows):
    n_items = dst_shift.shape[0] - 1
    d = w_up.shape[1]
    ff2 = w_up.shape[2]
    ff = w_down.shape[1]
    tr = TOKEN_TILE_ROWS
    smem = lambda imap: pl.BlockSpec((1, 1, MOE_ROWS), imap, memory_space=pltpu.SMEM)
    wspec = lambda shape: pl.BlockSpec(shape, lambda w, be, jb, nu: (be[w], 0, 0))
    grid_spec = pltpu.PrefetchScalarGridSpec(
        num_scalar_prefetch=3,
        grid=(n_items,),
        in_specs=[smem(lambda w, be, jb, nu: (jb[w], 0, 0)),
                  smem(lambda w, be, jb, nu: (jb[jnp.minimum(w + 1, n_items - 1)], 0, 0)),
                  smem(lambda w, be, jb, nu: (w, 0, 0)),
                  smem(lambda w, be, jb, nu: (w + 1, 0, 0)),
                  pl.BlockSpec(memory_space=pl.ANY),
                  wspec((1, d, ff2)), wspec((1, 1, ff2)), wspec((1, ff, d)), wspec((1, 1, d))],
        out_specs=pl.BlockSpec(memory_space=pl.ANY),
        scratch_shapes=[pltpu.VMEM((MOE_ROWS * tr, 128), F32)] * 4
                       + [pltpu.SemaphoreType.DMA((2,)), pltpu.SemaphoreType.DMA],
    )
    return pl.pallas_call(
        _moe_body,
        grid_spec=grid_spec,
        out_shape=jax.ShapeDtypeStruct((out_rows * tr, 128), F32),
        compiler_params=_cparams(("arbitrary",)),
        name="moe",
    )(item_e, item_blk, n_used, src_blk, src_blk, dst_shift, dst_shift, h2t, w_up, b_up, w_down, b_down)


def _combine_body(o0_ref, o1_ref, o2_ref, o3_ref, tw_ref, x1_ref, mod_ref, g_ref, y_ref):
    tw = tw_ref[...]
    t = tw.shape[0]
    lane = lax.broadcasted_iota(jnp.int32, tw.shape, 1)
    acc = None
    for kk, o_ref in enumerate((o0_ref, o1_ref, o2_ref, o3_ref)):
        wk = jnp.sum(jnp.where(lane == kk, tw, 0.0), axis=-1, keepdims=True)
        terms = [p * wk for p in _load_token_tiles(o_ref, t)]
        acc = terms if acc is None else [a + b for a, b in zip(acc, terms)]
    d = x1_ref.shape[1]
    ms = sum(jnp.sum(a * a, axis=-1, keepdims=True) for a in acc) * (1.0 / d)
    inv = lax.rsqrt(ms + EPS)
    for s, a in enumerate(acc):
        ls = slice(s * 128, (s + 1) * 128)
        y_ref[:, ls] = x1_ref[:, ls] + mod_ref[0, 0, :, ls] * (a * inv * g_ref[:, ls])


def _combine(moe_out, tw, x1, mods, g_post_ffn, n_pad, row0, rows, tile, tiles_per_mod):
    d = x1.shape[1]
    r = mods.shape[2]
    assert row0 % tile == 0 and n_pad % tile == 0 and rows % tile == 0
    t0 = row0 // tile
    kt = n_pad // tile
    ospec = lambda kk: pl.BlockSpec((tile * TOKEN_TILE_ROWS, 128), lambda i: (kk * kt + t0 + i, 0))
    return pl.pallas_call(
        _combine_body,
        grid=(rows // tile,),
        in_specs=[ospec(0), ospec(1), ospec(2), ospec(3),
                  pl.BlockSpec((tile, 128), lambda i: (t0 + i, 0)),
                  pl.BlockSpec((tile, d), lambda i: (t0 + i, 0)),
                  pl.BlockSpec((1, 1, r, d), lambda i: (0, i // tiles_per_mod, 0, 0)),
                  pl.BlockSpec((1, d), lambda i: (0, 0))],
        out_specs=pl.BlockSpec((tile, d), lambda i: (i, 0)),
        out_shape=jax.ShapeDtypeStruct((rows, d), F32),
        compiler_params=_cparams(("parallel",)), name="combine",
    )(moe_out, moe_out, moe_out, moe_out, tw, x1, mods, g_post_ffn)


def _rope_table(pos):
    inv = 1.0 / (ROPE_THETA ** (jnp.arange(0, MLA_DROPE, 2, dtype=F32) / MLA_DROPE))
    ang = pos.astype(F32)[:, None] * inv[None, :]
    c, s = jnp.cos(ang), jnp.sin(ang)
    return jnp.concatenate([c, c, s, s], axis=-1)


def _rotate_cols(w):
    half = w.shape[-1] // 2
    return jnp.concatenate([-w[..., half:], w[..., :half]], axis=-1)


def _prep_weights(w_in, b_in, ml_f_bias, mla_g_q, w_uq, mla_g_kv, w_uk, w_uv, g_pre_mix):
    qk_w = ML_HEADS * ML_DQK
    v_w = ML_HEADS * ML_DV
    sizes = [qk_w, qk_w, v_w, ML_HEADS, ML_HEADS, v_w, MLA_Q_RANK, MLA_KV_RANK, MLA_DROPE]
    pts = [int(p) for p in np.cumsum(sizes)[:-1]]
    wq, wk, wv, wi, wf, wo, wql, wkv, wpe = jnp.split(w_in, pts, axis=-1)
    bq, bk, bv, bi, bf, bo, bql, bkv, bpe = jnp.split(b_in, pts, axis=-1)
    d = w_in.shape[0]
    pad_w = jnp.zeros((d, 128 - 2 * ML_HEADS), F32)
    pad_b = jnp.zeros((128 - 2 * ML_HEADS,), F32)
    w_small = jnp.concatenate([wql, wkv, wpe, _rotate_cols(wpe), wi, wf, pad_w], axis=-1)
    b_small = jnp.concatenate([bql, bkv, bpe, _rotate_cols(bpe), bi, bf, pad_b], axis=-1)
    f_bias = jnp.concatenate([jnp.zeros((ML_HEADS,), F32), ml_f_bias, pad_b], axis=-1)
    wuq_h = w_uq.reshape(MLA_Q_RANK, MLA_HEADS, MLA_DNOPE + MLA_DROPE)
    wuq_r = wuq_h[..., MLA_DNOPE:]
    wuq_ext = jnp.concatenate([wuq_h, _rotate_cols(wuq_r)], axis=-1).reshape(MLA_Q_RANK, MLA_HEADS * QK_PAD)
    return {
        "g_pre_mix": g_pre_mix[None, :],
        "w_main": jnp.concatenate([wq, wk, wv, wo], axis=-1).astype(BF16),
        "b_main": jnp.concatenate([bq, bk, bv, bo], axis=-1)[None, :],
        "w_small": w_small.astype(BF16), "b_small": b_small[None, :], "f_bias": f_bias[None, :],
        "g_q": mla_g_q[None, :], "g_kv": mla_g_kv[None, :],
        "w_uq": wuq_ext.astype(BF16), "w_uk": w_uk.astype(BF16), "w_uv": w_uv.astype(BF16),
        "w_uv_t": w_uv.reshape(MLA_KV_RANK, MLA_HEADS, MLA_DV).transpose(1, 2, 0).astype(BF16),
    }


def _route_items(te, n_tok, n_pad):
    n_assign = n_tok * TOP_K
    n_blocks = -(-n_assign // MOE_ROWS)
    n_items = n_blocks + N_EXPERTS - 1
    i32 = jnp.int32
    e_flat = te.reshape(-1)
    skey = jnp.sort(e_flat * (1 << ORDER_BITS) + jnp.arange(n_assign, dtype=i32))
    order = skey & ((1 << ORDER_BITS) - 1)
    start = jnp.sum(e_flat[None, :] < jnp.arange(N_EXPERTS + 1, dtype=i32)[:, None], axis=1).astype(i32)
    lo_e, hi_e = start[:-1], start[1:]
    first_blk = lo_e // MOE_ROWS
    n_it = jnp.where(hi_e > lo_e, (hi_e - 1) // MOE_ROWS - first_blk + 1, 0)
    it_end = jnp.cumsum(n_it)
    n_used = it_end[-1]
    w = jnp.minimum(jnp.arange(n_items, dtype=i32), n_used - 1)
    item_e = jnp.minimum(jnp.searchsorted(it_end, w, side="right", method="compare_all"), N_EXPERTS - 1).astype(i32)
    item_blk = first_blk[item_e] + w - (it_end - n_it)[item_e]
    r = jnp.arange(MOE_ROWS, dtype=i32)[None, :]
    pos = item_blk[:, None] * MOE_ROWS + r
    mine = jnp.logical_and(pos >= lo_e[item_e][:, None], pos < hi_e[item_e][:, None])
    order_p = jnp.pad(order, (0, n_blocks * MOE_ROWS - n_assign)).reshape(n_blocks, MOE_ROWS)
    src_blk = order_p >> 2
    dst_blk = (order_p & 3) * n_pad + src_blk
    spare = TOP_K * n_pad + r
    dst_item = jnp.where(mine, dst_blk[item_blk], spare)
    dst_shift = jnp.concatenate([spare, dst_item], axis=0)
    tr = TOKEN_TILE_ROWS
    return item_e, item_blk, n_used.reshape(1).astype(i32), (src_blk * tr)[:, None, :], (dst_shift * tr)[:, None, :]


def kernel(x_prompt, x_sample, cache_kv_latent, cache_k_rope, state_mlstm_C, state_mlstm_n, state_mlstm_m,
           c_prompt, c_sample, w_ada, b_ada, g_pre_mix, g_post_mix, g_pre_ffn, g_post_ffn, w_in, b_in,
           ml_f_bias, ml_head_g, mla_g_q, w_uq, mla_g_kv, w_uk, w_uv, w_br_ml, w_br_mla, w_gate, b_gate,
           w_out, w_router, b_router, w_up, b_up, w_down, b_down):
    assert w_ada.shape[0] == 1, "single-layer step"
    bp, sp, d = x_prompt.shape
    bs, ss, _ = x_sample.shape
    past = cache_kv_latent.shape[2]
    n_p, n_s = bp * sp, bs * ss
    n_tok = n_p + n_s
    tile_p = TOK_TILE
    tile_s = min(TOK_TILE, n_s)
    n_pad = -(-n_tok // tile_p) * tile_p
    assert n_tok * TOP_K < (1 << ORDER_BITS) and d == TOKEN_TILE_ROWS * 128
    assert sp % ATT_TILE == 0 and sp % ML_CHUNK == 0 and sp % tile_p == 0
    assert n_s % tile_s == 0 and n_p % tile_s == 0 and n_pad % tile_s == 0

    mods = _ada(jnp.concatenate([c_prompt, c_sample], axis=0), w_ada[0], b_ada[0][None, :])
    mods = mods.reshape(bp + bs, 6, d).transpose(1, 0, 2)
    mods_p = mods[:, :bp][:, :, None, :]
    mods_s = jnp.broadcast_to(mods[:, bp:, None, :], (6, bs, ss, d)).reshape(6, n_s // tile_s, tile_s, d)

    wm = _prep_weights(w_in[0], b_in[0], ml_f_bias[0], mla_g_q[0], w_uq[0], mla_g_kv[0], w_uk[0], w_uv[0],
                       g_pre_mix[0])
    cs_p = _rope_table(jnp.arange(sp))
    cs_s = jnp.tile(_rope_table(past + jnp.arange(ss)), (bs, 1))

    xp = x_prompt.reshape(n_p, d)
    xs = x_sample.reshape(n_s, d)
    tiles_per_stream = sp // tile_p
    mp = _mixer_in(xp, mods_p[:2], cs_p, wm, tile_p, tiles_per_stream, tiles_per_stream)
    ms = _mixer_in(xs, mods_s[:2], cs_s, wm, tile_s, 1, n_s // tile_s)

    g_head = ml_head_g[0][None, :]

    def gate_rows(g, b, s):
        return g[:, :2 * ML_HEADS].reshape(b, s, 2 * ML_HEADS).transpose(0, 2, 1)

    qp, kp, vp, op, gp, lat_p, kr_p, qcat_p, kcat_p, vv_p = mp
    zc = jnp.zeros((bp, ML_HEADS, ML_DQK, ML_DV), F32)
    zn = jnp.zeros((bp, ML_HEADS, 1, ML_DQK), F32)
    zm = jnp.zeros((bp, ML_HEADS, 1, 1), F32)
    yml_p, c_p, nn_p, m_p = _mlstm(qp, kp, vp, op, gp, gate_rows(gp, bp, sp), zc, zn, zm, g_head, ML_CHUNK)
    ymla_p = _flash(qcat_p, kcat_p, vv_p, bp, sp, ATT_TILE)

    qs, ks, vs, os_, gs, lat_s, kr_s, qcat_s, kcat_s, _ = ms
    yml_s, c_s, nn_s, m_s = _mlstm(qs, ks, vs, os_, gs, gate_rows(gs, bs, ss),
                                   state_mlstm_C[0], state_mlstm_n[0][:, :, None, :],
                                   state_mlstm_m[0][:, :, None, None], g_head, ss)
    ckr_pad = jnp.pad(cache_k_rope[0], ((0, 0), (0, 0), (0, 128 - MLA_DROPE)))
    ymla_s = _sattn(qcat_s, cache_kv_latent[0], lat_s, ckr_pad, kcat_s, wm["w_uk"], wm["w_uv"], ss)

    w_r = jnp.pad(w_router[0], ((0, 0), (0, 128 - N_EXPERTS)))
    w_r_hi = w_r.astype(BF16)
    wt = {
        "g_pre_mix": g_pre_mix[0][None, :], "g_post_mix": g_post_mix[0][None, :], "g_pre_ffn": g_pre_ffn[0][None, :],
        "w_gate": w_gate[0].astype(BF16), "b_gate": b_gate[0][None, :],
        "w_br_ml": w_br_ml[0].astype(BF16), "w_br_mla": w_br_mla[0].astype(BF16), "w_out": w_out[0].astype(BF16),
        "w_router_hi": w_r_hi, "w_router_lo": (w_r - w_r_hi.astype(F32)).astype(BF16),
        "b_router": jnp.pad(b_router[0], (0, 128 - N_EXPERTS))[None, :],
    }
    merged = _tail(xp, yml_p, ymla_p, mods_p[:5], wt, tile_p, tiles_per_stream, n_tok)
    x1, h2, te, tw = _tail(xs, yml_s, ymla_s, mods_s[:5], wt, tile_s, 1, n_tok, row0=n_p, into=merged)
    te = te[:, :TOP_K]

    item_e, item_blk, n_used, src_blk, dst_shift = _route_items(te, n_tok, n_pad)
    moe_out = _moe(item_e, item_blk, n_used, src_blk, dst_shift, h2, w_up[0].astype(BF16), b_up[0][:, None, :],
                   w_down[0].astype(BF16), b_down[0][:, None, :], n_pad * TOP_K + MOE_ROWS)
    gff = g_post_ffn[0][None, :]
    y_p = _combine(moe_out, tw, x1, mods_p[5:], gff, n_pad, 0, n_p, tile_p, tiles_per_stream)
    y_s = _combine(moe_out, tw, x1, mods_s[5:], gff, n_pad, n_p, n_s, tile_s, 1)

    return (y_p.reshape(bp, sp, d), y_s.reshape(bs, ss, d),
            lat_p.reshape(1, bp, sp, MLA_KV_RANK), kr_p.reshape(1, bp, sp, MLA_DROPE),
            c_p[None], nn_p.reshape(1, bp, ML_HEADS, ML_DQK), m_p.reshape(1, bp, ML_HEADS),
            lat_s.reshape(1, bs, ss, MLA_KV_RANK), kr_s.reshape(1, bs, ss, MLA_DROPE),
            c_s[None], nn_s.reshape(1, bs, ML_HEADS, ML_DQK), m_s.reshape(1, bs, ML_HEADS))
```

```python
import functools

import jax
import jax.numpy as jnp
import numpy as np
from jax import lax
from jax.experimental import pallas as pl
from jax.experimental.pallas import tpu as pltpu

F32 = jnp.float32
BF16 = jnp.bfloat16

EPS = 1e-6
ROPE_THETA = 10000.0
SWIGLU_ALPHA = 1.702
SWIGLU_LIMIT = 7.0
TOP_K = 4

ML_HEADS = 4
ML_DQK = 128
ML_DV = 256
MLA_HEADS = 8
MLA_DNOPE = 128
MLA_DROPE = 64
MLA_DV = 128
MLA_Q_RANK = 384
MLA_KV_RANK = 256
MLA_SCALE = (MLA_DNOPE + MLA_DROPE) ** -0.5
Q_SCALE = MLA_SCALE * float(np.log2(np.e))
QK_PAD = 256
N_EXPERTS = 32

VMEM_LIMIT_BYTES = 52 * 1024 * 1024

TOK_TILE = 512
ML_CHUNK = 256
ATT_TILE = 1024
FLASH_DIAG_CHUNK = 512
FLASH_UNROLL = 2
STREAM_CHUNK = 64
MOE_ROWS = 512
TOKEN_TILE_ROWS = 8
ORDER_BITS = 18


def _cparams(sem, vmem=VMEM_LIMIT_BYTES):
    return pltpu.CompilerParams(dimension_semantics=sem, vmem_limit_bytes=vmem)


def _resident(a):
    return pl.BlockSpec(a.shape, lambda i: (0,) * a.ndim, pipeline_mode=pl.Buffered(1))


def _dot(a, b):
    return jnp.dot(a, b, preferred_element_type=F32)


def _dot_nt(a, b):
    return lax.dot_general(a, b, (((1,), (1,)), ((), ())), preferred_element_type=F32)


def _rms(x, g):
    return x * lax.rsqrt(jnp.mean(x * x, axis=-1, keepdims=True) + EPS) * g


def _store_token_tiles(ref, x):
    t = x.shape[0]
    for s in range(TOKEN_TILE_ROWS):
        ref[pl.ds(s, t, stride=TOKEN_TILE_ROWS), :] = x[:, s * 128:(s + 1) * 128]


def _load_token_tiles(ref, t):
    return [ref[pl.ds(s, t, stride=TOKEN_TILE_ROWS), :] for s in range(TOKEN_TILE_ROWS)]


def _split3(x):
    hi = x.astype(BF16)
    r1 = x - hi.astype(F32)
    mid = r1.astype(BF16)
    lo = (r1 - mid.astype(F32)).astype(BF16)
    return hi, mid, lo


def _ada_body(c_ref, w_ref, b_ref, o_ref):
    o_ref[...] = _dot(c_ref[...].astype(BF16), w_ref[...].astype(BF16)) + b_ref[...]


def _ada(c, w, b):
    m, d = c.shape
    n = w.shape[1]
    return pl.pallas_call(
        _ada_body,
        grid=(n // d,),
        in_specs=[pl.BlockSpec((m, d), lambda j: (0, 0)),
                  pl.BlockSpec((d, d), lambda j: (0, j)),
                  pl.BlockSpec((1, d), lambda j: (0, j))],
        out_specs=pl.BlockSpec((m, d), lambda j: (0, j)),
        out_shape=jax.ShapeDtypeStruct((m, n), F32),
        compiler_params=_cparams(("parallel",)), name="ada",
    )(c, w, b)


_SM_QLAT = 0
_SM_KVLAT = MLA_Q_RANK
_SM_KPE = MLA_Q_RANK + MLA_KV_RANK
_SM_GATE = _SM_KPE + 128
_SM_W = _SM_GATE + 128


def _mixer_in_body(x_ref, mod_ref, gpre_ref, wm_ref, bm_ref, ws_ref, bs_ref, fb_ref, gq_ref, gkv_ref,
                   wuq_ref, wuk_ref, wuvt_ref, cs_ref,
                   q_ref, k_ref, v_ref, o_ref, gate_ref, lat_ref, kr_ref, qcat_ref, kcat_ref, vvt_ref):
    x = x_ref[...]
    shift = mod_ref[0, 0]
    scale = mod_ref[1, 0]
    h = _rms(x, gpre_ref[...]) * (1.0 + scale) + shift
    hb = h.astype(BF16)
    qk_w = ML_HEADS * ML_DQK
    v_w = ML_HEADS * ML_DV
    q_ref[...] = (_dot(hb, wm_ref[:, 0:qk_w]) + bm_ref[:, 0:qk_w]).astype(BF16)
    k_ref[...] = ((_dot(hb, wm_ref[:, qk_w:2 * qk_w]) + bm_ref[:, qk_w:2 * qk_w]) * (ML_DQK ** -0.5)).astype(BF16)
    v_ref[...] = (_dot(hb, wm_ref[:, 2 * qk_w:2 * qk_w + v_w]) + bm_ref[:, 2 * qk_w:2 * qk_w + v_w]).astype(BF16)
    o_ref[...] = (_dot(hb, wm_ref[:, 2 * qk_w + v_w:]) + bm_ref[:, 2 * qk_w + v_w:]).astype(BF16)

    zs = _dot(hb, ws_ref[...]) + bs_ref[...]
    g = zs[:, _SM_GATE:_SM_GATE + 128]
    gf = g + fb_ref[...]
    logsig = jnp.minimum(gf, 0.0) - jnp.log1p(jnp.exp(-jnp.abs(gf)))
    lane = lax.broadcasted_iota(jnp.int32, g.shape, 1)
    gate_ref[...] = jnp.where(lane >= ML_HEADS, logsig, g)

    cs = cs_ref[...]
    qln = _rms(zs[:, _SM_QLAT:_SM_QLAT + MLA_Q_RANK], gq_ref[...]).astype(BF16)
    cq = _dot(qln, wuq_ref[...])
    for hd in range(MLA_HEADS):
        blk = cq[:, hd * QK_PAD:(hd + 1) * QK_PAD]
        t = blk[:, MLA_DNOPE:] * cs
        rot = t + pltpu.roll(t, 64, 1)
        qcat_ref[hd, :, 0:MLA_DNOPE] = (blk[:, :MLA_DNOPE] * Q_SCALE).astype(BF16)
        qcat_ref[hd, :, MLA_DNOPE:] = (rot * Q_SCALE).astype(BF16)
    latent = _rms(zs[:, _SM_KVLAT:_SM_KVLAT + MLA_KV_RANK], gkv_ref[...])
    lat_ref[...] = latent
    tk = zs[:, _SM_KPE:_SM_KPE + 128] * cs
    rotk = tk + pltpu.roll(tk, 64, 1)
    kr_ref[...] = rotk[:, 0:MLA_DROPE]
    krz = jnp.where(lane < MLA_DROPE, rotk, 0.0).astype(BF16)
    latb = latent.astype(BF16)
    kn = _dot(latb, wuk_ref[...])
    for hd in range(MLA_HEADS):
        kcat_ref[hd, :, 0:MLA_DNOPE] = kn[:, hd * MLA_DNOPE:(hd + 1) * MLA_DNOPE].astype(BF16)
        kcat_ref[hd, :, MLA_DNOPE:] = krz
        vvt_ref[hd] = _dot_nt(wuvt_ref[hd], latb).astype(BF16)


def _mixer_in(x, mods, cs, w, tile, tiles_per_mod, cs_tiles):
    n, d = x.shape
    nm, _, r, _ = mods.shape
    grid = (n // tile,)
    const = _resident
    tok = lambda width: pl.BlockSpec((tile, width), lambda i: (i, 0))
    head = lambda width: pl.BlockSpec((MLA_HEADS, tile, width), lambda i: (0, i, 0))
    qk_w = ML_HEADS * ML_DQK
    v_w = ML_HEADS * ML_DV
    out_shape = (
        jax.ShapeDtypeStruct((n, qk_w), BF16), jax.ShapeDtypeStruct((n, qk_w), BF16),
        jax.ShapeDtypeStruct((n, v_w), BF16), jax.ShapeDtypeStruct((n, v_w), BF16),
        jax.ShapeDtypeStruct((n, 128), F32),
        jax.ShapeDtypeStruct((n, MLA_KV_RANK), F32), jax.ShapeDtypeStruct((n, MLA_DROPE), F32),
        jax.ShapeDtypeStruct((MLA_HEADS, n, QK_PAD), BF16), jax.ShapeDtypeStruct((MLA_HEADS, n, QK_PAD), BF16),
        jax.ShapeDtypeStruct((MLA_HEADS, MLA_DV, n), BF16),
    )
    out_specs = (tok(qk_w), tok(qk_w), tok(v_w), tok(v_w), tok(128), tok(MLA_KV_RANK), tok(MLA_DROPE),
                 head(QK_PAD), head(QK_PAD), pl.BlockSpec((MLA_HEADS, MLA_DV, tile), lambda i: (0, 0, i)))
    in_specs = [
        tok(d),
        pl.BlockSpec((nm, 1, r, d), lambda i: (0, i // tiles_per_mod, 0, 0)),
        const(w["g_pre_mix"]), const(w["w_main"]), const(w["b_main"]), const(w["w_small"]), const(w["b_small"]),
        const(w["f_bias"]), const(w["g_q"]), const(w["g_kv"]), const(w["w_uq"]), const(w["w_uk"]), const(w["w_uv_t"]),
        pl.BlockSpec((tile, 128), lambda i: (i % cs_tiles, 0)),
    ]
    return pl.pallas_call(
        _mixer_in_body, grid=grid, in_specs=in_specs, out_specs=out_specs, out_shape=out_shape,
        compiler_params=_cparams(("parallel",)), name="mixer_in",
    )(x, mods, w["g_pre_mix"], w["w_main"], w["b_main"], w["w_small"], w["b_small"], w["f_bias"],
      w["g_q"], w["g_kv"], w["w_uq"], w["w_uk"], w["w_uv_t"], cs)


def _mlstm_body(q_ref, k_ref, v_ref, o_ref, gc_ref, gr_ref, c0_ref, n0_ref, m0_ref, gh_ref,
                y_ref, c_ref, n_ref, m_ref, br_ref):
    chunk = q_ref.shape[0]

    @pl.when(pl.program_id(1) == 0)
    def _():
        c_ref[...] = c0_ref[...]
        n_ref[...] = n0_ref[...]
        m_ref[...] = m0_ref[...]

    row = lax.broadcasted_iota(jnp.int32, (chunk, chunk), 0)
    col = lax.broadcasted_iota(jnp.int32, (chunk, chunk), 1)
    causal = col <= row
    tri = causal.astype(BF16)
    tri_t = (row <= col).astype(BF16)

    gc = gc_ref[...]
    gr = gr_ref[0]
    bc_all = sum(_dot(tri, p) for p in _split3(gc))
    br_ref[...] = sum(_dot(p, tri_t) for p in _split3(gr))

    lane = lax.broadcasted_iota(jnp.int32, gc.shape, 1)
    rowv = lax.broadcasted_iota(jnp.int32, (chunk, 1), 0)
    is_last = rowv == chunk - 1

    def pick(a, j):
        return jnp.sum(jnp.where(lane == j, a, 0.0), axis=-1, keepdims=True)

    def last(a):
        return jnp.sum(jnp.where(is_last, a, 0.0), axis=0, keepdims=True)

    for hd in range(ML_HEADS):
        qs = slice(hd * ML_DQK, (hd + 1) * ML_DQK)
        vs = slice(hd * ML_DV, (hd + 1) * ML_DV)
        q = q_ref[:, qs]
        k = k_ref[:, qs]
        v = v_ref[:, vs]
        ig_c = pick(gc, hd)
        b_c = pick(bc_all, ML_HEADS + hd)
        ig_r = gr_ref[0, hd:hd + 1, :]
        b_r = br_ref[ML_HEADS + hd:ML_HEADS + hd + 1, :]
        m_prev = m_ref[0, hd]
        c_prev = c_ref[0, hd]
        n_prev = n_ref[0, hd]

        dmat = jnp.where(causal, b_c - b_r + ig_r, -jnp.inf)
        inter = b_c + m_prev
        m_t = jnp.maximum(inter, jnp.max(dmat, axis=-1, keepdims=True))
        w_intra = jnp.exp(dmat - m_t)
        w_inter = jnp.exp(inter - m_t)
        qk = _dot_nt(q, k) * w_intra
        num = w_inter * _dot(q, c_prev.astype(BF16)) + _dot(qk.astype(BF16), v)
        qn = jnp.sum(q.astype(F32) * n_prev, axis=-1, keepdims=True)
        den = w_inter * qn + jnp.sum(qk, axis=-1, keepdims=True)
        hcur = num / jnp.maximum(jnp.abs(den), jnp.exp(-m_t))
        hn = hcur * lax.rsqrt(jnp.mean(hcur * hcur, axis=-1, keepdims=True) + EPS)
        y_ref[:, vs] = (hn * gh_ref[:, vs] * jax.nn.sigmoid(o_ref[:, vs].astype(F32))).astype(BF16)

        m_new = last(m_t)
        w_prev = jnp.exp(last(inter) - m_new)
        w_end = jnp.exp(last(b_c) - b_c + ig_c - m_new)
        kw = k.astype(F32) * w_end
        c_ref[0, hd] = w_prev * c_prev + _dot(kw.T.astype(BF16), v)
        n_ref[0, hd] = w_prev * n_prev + jnp.sum(kw, axis=0, keepdims=True)
        m_ref[0, hd] = m_new


def _mlstm(q, k, v, o, gates_col, gates_row, c0, n0, m0, g_head, chunk):
    n = q.shape[0]
    b, _, s = gates_row.shape
    nc = s // chunk
    tok = lambda width: pl.BlockSpec((chunk, width), lambda i, c: (i * nc + c, 0))
    state = lambda a: pl.BlockSpec((1,) + a.shape[1:], lambda i, c: (i,) + (0,) * (a.ndim - 1))
    qk_w = ML_HEADS * ML_DQK
    v_w = ML_HEADS * ML_DV
    return pl.pallas_call(
        _mlstm_body,
        grid=(b, nc),
        in_specs=[tok(qk_w), tok(qk_w), tok(v_w), tok(v_w), tok(128),
                  pl.BlockSpec((1, 8, chunk), lambda i, c: (i, 0, c)),
                  state(c0), state(n0), state(m0),
                  pl.BlockSpec((1, v_w), lambda i, c: (0, 0))],
        out_specs=(tok(v_w), state(c0), state(n0), state(m0)),
        out_shape=(jax.ShapeDtypeStruct((n, v_w), BF16),
                   jax.ShapeDtypeStruct(c0.shape, F32), jax.ShapeDtypeStruct(n0.shape, F32),
                   jax.ShapeDtypeStruct(m0.shape, F32)),
        scratch_shapes=[pltpu.VMEM((8, chunk), F32)],
        compiler_params=_cparams(("parallel", "arbitrary")), name="mlstm",
    )(q, k, v, o, gates_col, gates_row, c0, n0, m0, g_head)


def _flash_body(q_ref, k_ref, vt_ref, o_ref, m_sc, l_sc, acc_sc):
    tile = q_ref.shape[1]
    qi = pl.program_id(2)
    m_sc[...] = jnp.full(m_sc.shape, -jnp.inf, F32)
    l_sc[...] = jnp.zeros(l_sc.shape, F32)
    acc_sc[...] = jnp.zeros(acc_sc.shape, F32)

    def update(koff, ksize, q0, masked):
        qs = slice(q0, tile)
        kt = k_ref[0, pl.ds(koff, ksize), :]
        vt = vt_ref[0, :, pl.ds(koff, ksize)]
        st = _dot_nt(kt, q_ref[0, qs, :])
        if masked:
            shift = STREAM_CHUNK.bit_length() - 1
            kc = lax.broadcasted_iota(jnp.int32, st.shape, 0) >> shift
            qc = lax.broadcasted_iota(jnp.int32, st.shape, 1) >> shift
            st = jnp.where(kc <= qc, st, -jnp.inf)
        m_old = m_sc[:, qs]
        m_new = jnp.maximum(m_old, jnp.max(st, axis=0, keepdims=True))
        alpha = jnp.exp2(m_old - m_new)
        p = jnp.exp2(st - m_new)
        l_sc[:, qs] = alpha * l_sc[:, qs] + jnp.sum(p, axis=0, keepdims=True)
        acc_sc[:, qs] = alpha * acc_sc[:, qs] + _dot(vt, p.astype(BF16))
        m_sc[:, qs] = m_new

    def step(j, masked):
        off = pl.multiple_of(j * tile, tile)
        if not masked:
            update(off, tile, 0, False)
        else:
            for k0 in range(0, tile, FLASH_DIAG_CHUNK):
                update(off + k0, FLASH_DIAG_CHUNK, k0, True)

    def body(i, carry):
        for u in range(FLASH_UNROLL):
            step(FLASH_UNROLL * i + u, False)
        return carry

    def rest(j, carry):
        step(j, False)
        return carry

    n_full = qi // FLASH_UNROLL
    lax.fori_loop(0, n_full, body, 0)
    lax.fori_loop(n_full * FLASH_UNROLL, qi, rest, 0)
    step(qi, True)
    o_ref[...] = (acc_sc[...] / l_sc[...]).T.astype(BF16)


def _flash(qcat, kcat, vvt, batch, seq, tile):
    heads, n, _ = qcat.shape
    nq = seq // tile
    return pl.pallas_call(
        _flash_body,
        grid=(batch, heads, nq),
        in_specs=[pl.BlockSpec((1, tile, QK_PAD), lambda b, h, i: (h, b * nq + i, 0)),
                  pl.BlockSpec((1, seq, QK_PAD), lambda b, h, i: (h, b, 0)),
                  pl.BlockSpec((1, MLA_DV, seq), lambda b, h, i: (h, 0, b))],
        out_specs=pl.BlockSpec((tile, MLA_DV), lambda b, h, i: (b * nq + i, h)),
        out_shape=jax.ShapeDtypeStruct((n, heads * MLA_DV), BF16),
        scratch_shapes=[pltpu.VMEM((1, tile), F32), pltpu.VMEM((1, tile), F32), pltpu.VMEM((MLA_DV, tile), F32)],
        compiler_params=_cparams(("parallel", "parallel", "arbitrary")),
        name="flash",
    )(qcat, kcat, vvt)


def _sattn_body(q_ref, clat_ref, nlat_ref, ckr_ref, nk_ref, wuk_ref, wuv_ref, y_ref, qa_sc, qr_sc):
    s_new = nlat_ref.shape[0]
    for hd in range(MLA_HEADS):
        qh = q_ref[hd]
        hs = slice(hd * MLA_DNOPE, (hd + 1) * MLA_DNOPE)
        qa_sc[hd * s_new:(hd + 1) * s_new, :] = _dot_nt(qh[:, 0:MLA_DNOPE], wuk_ref[:, hs]).astype(BF16)
        qr_sc[hd * s_new:(hd + 1) * s_new, :] = qh[:, MLA_DNOPE:]
    qa = qa_sc[...]
    qr = qr_sc[...]
    clat = clat_ref[0].astype(BF16)
    nlat = nlat_ref[...].astype(BF16)
    ckr = ckr_ref[0].astype(BF16)
    nkr = nk_ref[0, :, MLA_DNOPE:]
    s_c = _dot_nt(qa, clat) + _dot_nt(qr, ckr)
    s_n = _dot_nt(qa, nlat) + _dot_nt(qr, nkr)
    m = jnp.maximum(jnp.max(s_c, axis=-1, keepdims=True), jnp.max(s_n, axis=-1, keepdims=True))
    p_c = jnp.exp2(s_c - m)
    p_n = jnp.exp2(s_n - m)
    l = jnp.sum(p_c, axis=-1, keepdims=True) + jnp.sum(p_n, axis=-1, keepdims=True)
    o_lat = ((_dot(p_c.astype(BF16), clat) + _dot(p_n.astype(BF16), nlat)) / l).astype(BF16)
    for hd in range(MLA_HEADS):
        hs = slice(hd * MLA_DV, (hd + 1) * MLA_DV)
        y_ref[:, hs] = _dot(o_lat[hd * s_new:(hd + 1) * s_new, :], wuv_ref[:, hs]).astype(BF16)


def _sattn(qcat, cache_lat, new_lat, cache_kr_pad, kcat, w_uk, w_uv, s_new):
    heads, n, _ = qcat.shape
    b, past, _ = cache_lat.shape
    const = _resident
    return pl.pallas_call(
        _sattn_body,
        grid=(b,),
        in_specs=[pl.BlockSpec((heads, s_new, QK_PAD), lambda i: (0, i, 0)),
                  pl.BlockSpec((1, past, MLA_KV_RANK), lambda i: (i, 0, 0)),
                  pl.BlockSpec((s_new, MLA_KV_RANK), lambda i: (i, 0)),
                  pl.BlockSpec((1, past, 128), lambda i: (i, 0, 0)),
                  pl.BlockSpec((1, s_new, QK_PAD), lambda i: (0, i, 0)),
                  const(w_uk), const(w_uv)],
        out_specs=pl.BlockSpec((s_new, heads * MLA_DV), lambda i: (i, 0)),
        out_shape=jax.ShapeDtypeStruct((n, heads * MLA_DV), BF16),
        scratch_shapes=[pltpu.VMEM((heads * s_new, MLA_KV_RANK), BF16), pltpu.VMEM((heads * s_new, 128), BF16)],
        compiler_params=_cparams(("parallel",)), name="sattn",
    )(qcat, cache_lat, new_lat, cache_kr_pad, kcat, w_uk, w_uv)


def _tail_body(x_ref, yml_ref, ymla_ref, mod_ref, gpre_ref, gpost_ref, gffn_ref, wg_ref, bg_ref,
               wml_ref, wmla_ref, wout_ref, wrh_ref, wrl_ref, br_ref,
               x1_ref, h2_ref, te_ref, tw_ref):
    x = x_ref[...]
    d = x.shape[1]
    sh1, sc1, gt1, sh2, sc2 = (mod_ref[j, 0] for j in range(5))
    hb = (_rms(x, gpre_ref[...]) * (1.0 + sc1) + sh1).astype(BF16)
    g = jax.nn.sigmoid(_dot(hb, wg_ref[...]) + bg_ref[...])
    yb = g[:, :d] * _dot(yml_ref[...], wml_ref[...]) + g[:, d:] * _dot(ymla_ref[...], wmla_ref[...])
    y = _dot(yb.astype(BF16), wout_ref[...])
    x1 = x + gt1 * _rms(y, gpost_ref[...])
    x1_ref[...] = x1
    h2 = _rms(x1, gffn_ref[...]) * (1.0 + sc2) + sh2
    _store_token_tiles(h2_ref, h2)
    hi = h2.astype(BF16)
    lo = (h2 - hi.astype(F32)).astype(BF16)
    logits = _dot(hi, wrh_ref[...]) + _dot(hi, wrl_ref[...]) + _dot(lo, wrh_ref[...]) + br_ref[...]
    lane = lax.broadcasted_iota(jnp.int32, logits.shape, 1)
    lane_f = lane.astype(F32)
    lg = jnp.where(lane < N_EXPERTS, logits, -jnp.inf)
    te = jnp.zeros(lg.shape, F32)
    tw = jnp.zeros(lg.shape, F32)
    top0 = None
    for kk in range(TOP_K):
        mx = jnp.max(lg, axis=-1, keepdims=True)
        idx = jnp.min(jnp.where(lg == mx, lane_f, 128.0), axis=-1, keepdims=True)
        if kk == 0:
            top0 = mx
        te = jnp.where(lane == kk, idx, te)
        tw = jnp.where(lane == kk, jnp.exp(mx - top0), tw)
        lg = jnp.where(lane_f == idx, -jnp.inf, lg)
    te_ref[...] = te.astype(jnp.int32)
    tw_ref[...] = tw / jnp.sum(tw, axis=-1, keepdims=True)


def _tail(x, yml, ymla, mods, w, tile, tiles_per_mod, n_total, row0=0, into=None):
    n, d = x.shape
    nm, _, r, _ = mods.shape
    assert nm == 5 and row0 % tile == 0
    t0 = row0 // tile
    const = _resident
    tok = lambda width: pl.BlockSpec((tile, width), lambda i: (i, 0))
    out = lambda rows, width: pl.BlockSpec((rows, width), lambda i: (t0 + i, 0))
    names = ("g_pre_mix", "g_post_mix", "g_pre_ffn", "w_gate", "b_gate", "w_br_ml", "w_br_mla", "w_out",
             "w_router_hi", "w_router_lo", "b_router")
    n_in = 4 + len(names)
    extra = () if into is None else tuple(into)

    def body(*refs):
        _tail_body(*refs[:n_in], *refs[n_in + len(extra):])

    return pl.pallas_call(
        body,
        grid=(n // tile,),
        in_specs=[tok(d), tok(d), tok(d),
                  pl.BlockSpec((5, 1, r, d), lambda i: (0, i // tiles_per_mod, 0, 0))]
                 + [const(w[k]) for k in names] + [pl.BlockSpec(memory_space=pl.ANY)] * len(extra),
        out_specs=(out(tile, d), out(tile * TOKEN_TILE_ROWS, 128), out(tile, 128), out(tile, 128)),
        out_shape=(jax.ShapeDtypeStruct((n_total, d), F32),
                   jax.ShapeDtypeStruct((n_total * TOKEN_TILE_ROWS, 128), F32),
                   jax.ShapeDtypeStruct((n_total, 128), jnp.int32), jax.ShapeDtypeStruct((n_total, 128), F32)),
        input_output_aliases={n_in + j: j for j in range(len(extra))},
        compiler_params=_cparams(("parallel",)), name="tail",
    )(x, yml, ymla, mods, *[w[k] for k in names], *extra)


def _moe_body(be_ref, jb_ref, nu_ref, src_ref, srcn_ref, dstp_ref, dst_ref, h2_hbm, wup_ref, bup_ref, wdn_ref,
              bdn_ref, out_hbm, xbuf0, xbuf1, obuf0, obuf1, wup_bf, wdn_bf, sem_in, sem_out):
    del jb_ref
    w = pl.program_id(0)
    n_used = nu_ref[0]
    rows = src_ref.shape[2]
    tr = TOKEN_TILE_ROWS
    ff = wdn_ref.shape[1]

    new_expert = jnp.logical_or(w == 0, be_ref[w] != be_ref[jnp.maximum(w - 1, 0)])

    @pl.when(jnp.logical_and(w < n_used, new_expert))
    def _():
        wup_bf[...] = wup_ref[0].astype(BF16)
        wdn_bf[...] = wdn_ref[0].astype(BF16)

    xbufs, obufs = (xbuf0, xbuf1), (obuf0, obuf1)

    def gather_copy(idx_ref, slot, i):
        return pltpu.make_async_copy(h2_hbm.at[pl.ds(pl.multiple_of(idx_ref[0, 0, i], tr), tr)],
                                     xbufs[slot].at[pl.ds(i * tr, tr)], sem_in.at[slot])

    def scatter_copy(idx_ref, slot, i):
        return pltpu.make_async_copy(obufs[slot].at[pl.ds(i * tr, tr)],
                                     out_hbm.at[pl.ds(pl.multiple_of(idx_ref[0, 0, i], tr), tr)], sem_out)

    def gather_wait(slot):
        pltpu.make_async_copy(h2_hbm.at[pl.ds(0, rows * tr)], xbufs[slot], sem_in.at[slot]).wait()

    def scatter_wait(slot):
        pltpu.make_async_copy(obufs[slot], out_hbm.at[pl.ds(0, rows * tr)], sem_out).wait()

    def rolled(copy, idx_ref, slot):
        def issue(i, c):
            copy(idx_ref, slot, i).start()
            return c
        lax.fori_loop(0, rows, issue, 0, unroll=8)

    def item(slot):
        other = 1 - slot
        @pl.when(w == 0)
        def _():
            rolled(gather_copy, src_ref, slot)
            obufs[other][...] = jnp.zeros(obufs[other].shape, F32)

        gather_wait(slot)

        @pl.when(w > 0)
        def _():
            scatter_wait(slot)

        x = jnp.concatenate(_load_token_tiles(xbufs[slot], rows), axis=-1).astype(BF16)
        for i in range(rows):
            gather_copy(srcn_ref, other, i).start()
            scatter_copy(dstp_ref, other, i).start()
        gu = _dot(x, wup_bf[...]) + bup_ref[0]
        gl = jnp.minimum(gu[:, :ff], SWIGLU_LIMIT)
        up = jnp.clip(gu[:, ff:], -SWIGLU_LIMIT, SWIGLU_LIMIT)
        act = (up + 1.0) * gl * jax.nn.sigmoid(SWIGLU_ALPHA * gl)
        _store_token_tiles(obufs[slot], _dot(act.astype(BF16), wdn_bf[...]) + bdn_ref[0])

        @pl.when(w == n_used - 1)
        def _():
            gather_wait(other)
            scatter_wait(other)
            rolled(scatter_copy, dst_ref, slot)
            scatter_wait(slot)

    for slot in (1, 0):
        pl.when(jnp.logical_and(w < n_used, (w & 1) == slot))(functools.partial(item, slot))


def _moe(item_e, item_blk, n_used, src_blk, dst_shift, h2t, w_up, b_up, w_down, b_down, out_rows):
    n_items = dst_shift.shape[0] - 1
    d = w_up.shape[1]
    ff2 = w_up.shape[2]
    ff = w_down.shape[1]
    tr = TOKEN_TILE_ROWS
    smem = lambda imap: pl.BlockSpec((1, 1, MOE_ROWS), imap, memory_space=pltpu.SMEM)
    wspec = lambda shape: pl.BlockSpec(shape, lambda w, be, jb, nu: (be[w], 0, 0))
    grid_spec = pltpu.PrefetchScalarGridSpec(
        num_scalar_prefetch=3,
        grid=(n_items,),
        in_specs=[smem(lambda w, be, jb, nu: (jb[w], 0, 0)),
                  smem(lambda w, be, jb, nu: (jb[jnp.minimum(w + 1, n_items - 1)], 0, 0)),
                  smem(lambda w, be, jb, nu: (w, 0, 0)),
                  smem(lambda w, be, jb, nu: (w + 1, 0, 0)),
                  pl.BlockSpec(memory_space=pl.ANY),
                  wspec((1, d, ff2)), wspec((1, 1, ff2)), wspec((1, ff, d)), wspec((1, 1, d))],
        out_specs=pl.BlockSpec(memory_space=pl.ANY),
        scratch_shapes=[pltpu.VMEM((MOE_ROWS * tr, 128), F32)] * 4
                       + [pltpu.VMEM((d, ff2), BF16), pltpu.VMEM((ff, d), BF16),
                          pltpu.SemaphoreType.DMA((2,)), pltpu.SemaphoreType.DMA],
    )
    return pl.pallas_call(
        _moe_body,
        grid_spec=grid_spec,
        out_shape=jax.ShapeDtypeStruct((out_rows * tr, 128), F32),
        compiler_params=_cparams(("arbitrary",)),
        name="moe",
    )(item_e, item_blk, n_used, src_blk, src_blk, dst_shift, dst_shift, h2t, w_up, b_up, w_down, b_down)


def _combine_body(o0_ref, o1_ref, o2_ref, o3_ref, tw_ref, x1_ref, mod_ref, g_ref, y_ref):
    tw = tw_ref[...]
    t = tw.shape[0]
    lane = lax.broadcasted_iota(jnp.int32, tw.shape, 1)
    acc = None
    for kk, o_ref in enumerate((o0_ref, o1_ref, o2_ref, o3_ref)):
        wk = jnp.sum(jnp.where(lane == kk, tw, 0.0), axis=-1, keepdims=True)
        terms = [p * wk for p in _load_token_tiles(o_ref, t)]
        acc = terms if acc is None else [a + b for a, b in zip(acc, terms)]
    d = x1_ref.shape[1]
    ms = sum(jnp.sum(a * a, axis=-1, keepdims=True) for a in acc) * (1.0 / d)
    inv = lax.rsqrt(ms + EPS)
    for s, a in enumerate(acc):
        ls = slice(s * 128, (s + 1) * 128)
        y_ref[:, ls] = x1_ref[:, ls] + mod_ref[0, 0, :, ls] * (a * inv * g_ref[:, ls])


def _combine(moe_out, tw, x1, mods, g_post_ffn, n_pad, row0, rows, tile, tiles_per_mod):
    d = x1.shape[1]
    r = mods.shape[2]
    assert row0 % tile == 0 and n_pad % tile == 0 and rows % tile == 0
    t0 = row0 // tile
    kt = n_pad // tile
    ospec = lambda kk: pl.BlockSpec((tile * TOKEN_TILE_ROWS, 128), lambda i: (kk * kt + t0 + i, 0))
    return pl.pallas_call(
        _combine_body,
        grid=(rows // tile,),
        in_specs=[ospec(0), ospec(1), ospec(2), ospec(3),
                  pl.BlockSpec((tile, 128), lambda i: (t0 + i, 0)),
                  pl.BlockSpec((tile, d), lambda i: (t0 + i, 0)),
                  pl.BlockSpec((1, 1, r, d), lambda i: (0, i // tiles_per_mod, 0, 0)),
                  pl.BlockSpec((1, d), lambda i: (0, 0))],
        out_specs=pl.BlockSpec((tile, d), lambda i: (i, 0)),
        out_shape=jax.ShapeDtypeStruct((rows, d), F32),
        compiler_params=_cparams(("parallel",)), name="combine",
    )(moe_out, moe_out, moe_out, moe_out, tw, x1, mods, g_post_ffn)


def _rope_table(pos):
    inv = 1.0 / (ROPE_THETA ** (jnp.arange(0, MLA_DROPE, 2, dtype=F32) / MLA_DROPE))
    ang = pos.astype(F32)[:, None] * inv[None, :]
    c, s = jnp.cos(ang), jnp.sin(ang)
    return jnp.concatenate([c, c, s, s], axis=-1)


def _rotate_cols(w):
    half = w.shape[-1] // 2
    return jnp.concatenate([-w[..., half:], w[..., :half]], axis=-1)


def _prep_weights(w_in, b_in, ml_f_bias, mla_g_q, w_uq, mla_g_kv, w_uk, w_uv, g_pre_mix):
    qk_w = ML_HEADS * ML_DQK
    v_w = ML_HEADS * ML_DV
    sizes = [qk_w, qk_w, v_w, ML_HEADS, ML_HEADS, v_w, MLA_Q_RANK, MLA_KV_RANK, MLA_DROPE]
    pts = [int(p) for p in np.cumsum(sizes)[:-1]]
    wq, wk, wv, wi, wf, wo, wql, wkv, wpe = jnp.split(w_in, pts, axis=-1)
    bq, bk, bv, bi, bf, bo, bql, bkv, bpe = jnp.split(b_in, pts, axis=-1)
    d = w_in.shape[0]
    pad_w = jnp.zeros((d, 128 - 2 * ML_HEADS), F32)
    pad_b = jnp.zeros((128 - 2 * ML_HEADS,), F32)
    w_small = jnp.concatenate([wql, wkv, wpe, _rotate_cols(wpe), wi, wf, pad_w], axis=-1)
    b_small = jnp.concatenate([bql, bkv, bpe, _rotate_cols(bpe), bi, bf, pad_b], axis=-1)
    f_bias = jnp.concatenate([jnp.zeros((ML_HEADS,), F32), ml_f_bias, pad_b], axis=-1)
    wuq_h = w_uq.reshape(MLA_Q_RANK, MLA_HEADS, MLA_DNOPE + MLA_DROPE)
    wuq_r = wuq_h[..., MLA_DNOPE:]
    wuq_ext = jnp.concatenate([wuq_h, _rotate_cols(wuq_r)], axis=-1).reshape(MLA_Q_RANK, MLA_HEADS * QK_PAD)
    return {
        "g_pre_mix": g_pre_mix[None, :],
        "w_main": jnp.concatenate([wq, wk, wv, wo], axis=-1).astype(BF16),
        "b_main": jnp.concatenate([bq, bk, bv, bo], axis=-1)[None, :],
        "w_small": w_small.astype(BF16), "b_small": b_small[None, :], "f_bias": f_bias[None, :],
        "g_q": mla_g_q[None, :], "g_kv": mla_g_kv[None, :],
        "w_uq": wuq_ext.astype(BF16), "w_uk": w_uk.astype(BF16), "w_uv": w_uv.astype(BF16),
        "w_uv_t": w_uv.reshape(MLA_KV_RANK, MLA_HEADS, MLA_DV).transpose(1, 2, 0).astype(BF16),
    }


def _route_items(te, n_tok, n_pad):
    n_assign = n_tok * TOP_K
    n_blocks = -(-n_assign // MOE_ROWS)
    n_items = n_blocks + N_EXPERTS - 1
    i32 = jnp.int32
    e_flat = te.reshape(-1)
    skey = jnp.sort(e_flat * (1 << ORDER_BITS) + jnp.arange(n_assign, dtype=i32))
    order = skey & ((1 << ORDER_BITS) - 1)
    start = jnp.sum(e_flat[None, :] < jnp.arange(N_EXPERTS + 1, dtype=i32)[:, None], axis=1).astype(i32)
    lo_e, hi_e = start[:-1], start[1:]
    first_blk = lo_e // MOE_ROWS
    n_it = jnp.where(hi_e > lo_e, (hi_e - 1) // MOE_ROWS - first_blk + 1, 0)
    it_end = jnp.cumsum(n_it)
    n_used = it_end[-1]
    w = jnp.minimum(jnp.arange(n_items, dtype=i32), n_used - 1)
    item_e = jnp.minimum(jnp.searchsorted(it_end, w, side="right", method="compare_all"), N_EXPERTS - 1).astype(i32)
    item_blk = first_blk[item_e] + w - (it_end - n_it)[item_e]
    r = jnp.arange(MOE_ROWS, dtype=i32)[None, :]
    pos = item_blk[:, None] * MOE_ROWS + r
    mine = jnp.logical_and(pos >= lo_e[item_e][:, None], pos < hi_e[item_e][:, None])
    order_p = jnp.pad(order, (0, n_blocks * MOE_ROWS - n_assign)).reshape(n_blocks, MOE_ROWS)
    src_blk = order_p >> 2
    dst_blk = (order_p & 3) * n_pad + src_blk
    spare = TOP_K * n_pad + r
    dst_item = jnp.where(mine, dst_blk[item_blk], spare)
    dst_shift = jnp.concatenate([spare, dst_item], axis=0)
    tr = TOKEN_TILE_ROWS
    return item_e, item_blk, n_used.reshape(1).astype(i32), (src_blk * tr)[:, None, :], (dst_shift * tr)[:, None, :]


def kernel(x_prompt, x_sample, cache_kv_latent, cache_k_rope, state_mlstm_C, state_mlstm_n, state_mlstm_m,
           c_prompt, c_sample, w_ada, b_ada, g_pre_mix, g_post_mix, g_pre_ffn, g_post_ffn, w_in, b_in,
           ml_f_bias, ml_head_g, mla_g_q, w_uq, mla_g_kv, w_uk, w_uv, w_br_ml, w_br_mla, w_gate, b_gate,
           w_out, w_router, b_router, w_up, b_up, w_down, b_down):
    assert w_ada.shape[0] == 1, "single-layer step"
    bp, sp, d = x_prompt.shape
    bs, ss, _ = x_sample.shape
    past = cache_kv_latent.shape[2]
    n_p, n_s = bp * sp, bs * ss
    n_tok = n_p + n_s
    tile_p = TOK_TILE
    tile_s = min(TOK_TILE, n_s)
    n_pad = -(-n_tok // tile_p) * tile_p
    assert n_tok * TOP_K < (1 << ORDER_BITS) and d == TOKEN_TILE_ROWS * 128
    assert sp % ATT_TILE == 0 and sp % ML_CHUNK == 0 and sp % tile_p == 0
    assert n_s % tile_s == 0 and n_p % tile_s == 0 and n_pad % tile_s == 0

    mods = _ada(jnp.concatenate([c_prompt, c_sample], axis=0), w_ada[0], b_ada[0][None, :])
    mods = mods.reshape(bp + bs, 6, d).transpose(1, 0, 2)
    mods_p = mods[:, :bp][:, :, None, :]
    mods_s = jnp.broadcast_to(mods[:, bp:, None, :], (6, bs, ss, d)).reshape(6, n_s // tile_s, tile_s, d)

    wm = _prep_weights(w_in[0], b_in[0], ml_f_bias[0], mla_g_q[0], w_uq[0], mla_g_kv[0], w_uk[0], w_uv[0],
                       g_pre_mix[0])
    cs_p = _rope_table(jnp.arange(sp))
    cs_s = jnp.tile(_rope_table(past + jnp.arange(ss)), (bs, 1))

    xp = x_prompt.reshape(n_p, d)
    xs = x_sample.reshape(n_s, d)
    tiles_per_stream = sp // tile_p
    mp = _mixer_in(xp, mods_p[:2], cs_p, wm, tile_p, tiles_per_stream, tiles_per_stream)
    ms = _mixer_in(xs, mods_s[:2], cs_s, wm, tile_s, 1, n_s // tile_s)

    g_head = ml_head_g[0][None, :]

    def gate_rows(g, b, s):
        return g[:, :2 * ML_HEADS].reshape(b, s, 2 * ML_HEADS).transpose(0, 2, 1)

    qp, kp, vp, op, gp, lat_p, kr_p, qcat_p, kcat_p, vv_p = mp
    zc = jnp.zeros((bp, ML_HEADS, ML_DQK, ML_DV), F32)
    zn = jnp.zeros((bp, ML_HEADS, 1, ML_DQK), F32)
    zm = jnp.zeros((bp, ML_HEADS, 1, 1), F32)
    yml_p, c_p, nn_p, m_p = _mlstm(qp, kp, vp, op, gp, gate_rows(gp, bp, sp), zc, zn, zm, g_head, ML_CHUNK)
    ymla_p = _flash(qcat_p, kcat_p, vv_p, bp, sp, ATT_TILE)

    qs, ks, vs, os_, gs, lat_s, kr_s, qcat_s, kcat_s, _ = ms
    yml_s, c_s, nn_s, m_s = _mlstm(qs, ks, vs, os_, gs, gate_rows(gs, bs, ss),
                                   state_mlstm_C[0], state_mlstm_n[0][:, :, None, :],
                                   state_mlstm_m[0][:, :, None, None], g_head, ss)
    ckr_pad = jnp.pad(cache_k_rope[0], ((0, 0), (0, 0), (0, 128 - MLA_DROPE)))
    ymla_s = _sattn(qcat_s, cache_kv_latent[0], lat_s, ckr_pad, kcat_s, wm["w_uk"], wm["w_uv"], ss)

    w_r = jnp.pad(w_router[0], ((0, 0), (0, 128 - N_EXPERTS)))
    w_r_hi = w_r.astype(BF16)
    wt = {
        "g_pre_mix": g_pre_mix[0][None, :], "g_post_mix": g_post_mix[0][None, :], "g_pre_ffn": g_pre_ffn[0][None, :],
        "w_gate": w_gate[0].astype(BF16), "b_gate": b_gate[0][None, :],
        "w_br_ml": w_br_ml[0].astype(BF16), "w_br_mla": w_br_mla[0].astype(BF16), "w_out": w_out[0].astype(BF16),
        "w_router_hi": w_r_hi, "w_router_lo": (w_r - w_r_hi.astype(F32)).astype(BF16),
        "b_router": jnp.pad(b_router[0], (0, 128 - N_EXPERTS))[None, :],
    }
    merged = _tail(xp, yml_p, ymla_p, mods_p[:5], wt, tile_p, tiles_per_stream, n_tok)
    x1, h2, te, tw = _tail(xs, yml_s, ymla_s, mods_s[:5], wt, tile_s, 1, n_tok, row0=n_p, into=merged)
    te = te[:, :TOP_K]

    item_e, item_blk, n_used, src_blk, dst_shift = _route_items(te, n_tok, n_pad)
    moe_out = _moe(item_e, item_blk, n_used, src_blk, dst_shift, h2, w_up[0], b_up[0][:, None, :],
                   w_down[0], b_down[0][:, None, :], n_pad * TOP_K + MOE_ROWS)
    gff = g_post_ffn[0][None, :]
    y_p = _combine(moe_out, tw, x1, mods_p[5:], gff, n_pad, 0, n_p, tile_p, tiles_per_stream)
    y_s = _combine(moe_out, tw, x1, mods_s[5:], gff, n_pad, n_p, n_s, tile_s, 1)

    return (y_p.reshape(bp, sp, d), y_s.reshape(bs, ss, d),
            lat_p.reshape(1, bp, sp, MLA_KV_RANK), kr_p.reshape(1, bp, sp, MLA_DROPE),
            c_p[None], nn_p.reshape(1, bp, ML_HEADS, ML_DQK), m_p.reshape(1, bp, ML_HEADS),
            lat_s.reshape(1, bs, ss, MLA_KV_RANK), kr_s.reshape(1, bs, ss, MLA_DROPE),
            c_s[None], nn_s.reshape(1, bs, ML_HEADS, ML_DQK), m_s.reshape(1, bs, ML_HEADS))
```

```python
import functools

import jax
import jax.numpy as jnp
import numpy as np
from jax import lax
from jax.experimental import pallas as pl
from jax.experimental.pallas import tpu as pltpu

F32 = jnp.float32
BF16 = jnp.bfloat16

EPS = 1e-6
ROPE_THETA = 10000.0
SWIGLU_ALPHA = 1.702
SWIGLU_LIMIT = 7.0
TOP_K = 4

ML_HEADS = 4
ML_DQK = 128
ML_DV = 256
MLA_HEADS = 8
MLA_DNOPE = 128
MLA_DROPE = 64
MLA_DV = 128
MLA_Q_RANK = 384
MLA_KV_RANK = 256
MLA_SCALE = (MLA_DNOPE + MLA_DROPE) ** -0.5
Q_SCALE = MLA_SCALE * float(np.log2(np.e))
QK_PAD = 256
N_EXPERTS = 32

VMEM_LIMIT_BYTES = 52 * 1024 * 1024

TOK_TILE = 512
ML_CHUNK = 256
ATT_TILE = 1024
FLASH_DIAG_CHUNK = 512
FLASH_UNROLL = 3
STREAM_CHUNK = 64
MOE_ROWS = 512
TOKEN_TILE_ROWS = 8
ORDER_BITS = 18


def _cparams(sem, vmem=VMEM_LIMIT_BYTES):
    return pltpu.CompilerParams(dimension_semantics=sem, vmem_limit_bytes=vmem)


def _resident(a):
    return pl.BlockSpec(a.shape, lambda i: (0,) * a.ndim, pipeline_mode=pl.Buffered(1))


def _dot(a, b):
    return jnp.dot(a, b, preferred_element_type=F32)


def _dot_nt(a, b):
    return lax.dot_general(a, b, (((1,), (1,)), ((), ())), preferred_element_type=F32)


def _rms(x, g):
    return x * lax.rsqrt(jnp.mean(x * x, axis=-1, keepdims=True) + EPS) * g


def _store_token_tiles(ref, x):
    t = x.shape[0]
    for s in range(TOKEN_TILE_ROWS):
        ref[pl.ds(s, t, stride=TOKEN_TILE_ROWS), :] = x[:, s * 128:(s + 1) * 128]


def _load_token_tiles(ref, t):
    return [ref[pl.ds(s, t, stride=TOKEN_TILE_ROWS), :] for s in range(TOKEN_TILE_ROWS)]


def _split3(x):
    hi = x.astype(BF16)
    r1 = x - hi.astype(F32)
    mid = r1.astype(BF16)
    lo = (r1 - mid.astype(F32)).astype(BF16)
    return hi, mid, lo


def _ada_body(c_ref, w_ref, b_ref, o_ref):
    o_ref[...] = _dot(c_ref[...].astype(BF16), w_ref[...].astype(BF16)) + b_ref[...]


def _ada(c, w, b):
    m, d = c.shape
    n = w.shape[1]
    return pl.pallas_call(
        _ada_body,
        grid=(n // d,),
        in_specs=[pl.BlockSpec((m, d), lambda j: (0, 0)),
                  pl.BlockSpec((d, d), lambda j: (0, j)),
                  pl.BlockSpec((1, d), lambda j: (0, j))],
        out_specs=pl.BlockSpec((m, d), lambda j: (0, j)),
        out_shape=jax.ShapeDtypeStruct((m, n), F32),
        compiler_params=_cparams(("parallel",)), name="ada",
    )(c, w, b)


_SM_QLAT = 0
_SM_KVLAT = MLA_Q_RANK
_SM_KPE = MLA_Q_RANK + MLA_KV_RANK
_SM_GATE = _SM_KPE + 128
_SM_W = _SM_GATE + 128


def _mixer_in_body(x_ref, mod_ref, gpre_ref, wm_ref, bm_ref, ws_ref, bs_ref, fb_ref, gq_ref, gkv_ref,
                   wuq_ref, wuk_ref, wuvt_ref, cs_ref,
                   q_ref, k_ref, v_ref, o_ref, gate_ref, lat_ref, kr_ref, qcat_ref, kcat_ref, vvt_ref):
    x = x_ref[...]
    shift = mod_ref[0, 0]
    scale = mod_ref[1, 0]
    h = _rms(x, gpre_ref[...]) * (1.0 + scale) + shift
    hb = h.astype(BF16)
    qk_w = ML_HEADS * ML_DQK
    v_w = ML_HEADS * ML_DV
    q_ref[...] = (_dot(hb, wm_ref[:, 0:qk_w]) + bm_ref[:, 0:qk_w]).astype(BF16)
    k_ref[...] = ((_dot(hb, wm_ref[:, qk_w:2 * qk_w]) + bm_ref[:, qk_w:2 * qk_w]) * (ML_DQK ** -0.5)).astype(BF16)
    v_ref[...] = (_dot(hb, wm_ref[:, 2 * qk_w:2 * qk_w + v_w]) + bm_ref[:, 2 * qk_w:2 * qk_w + v_w]).astype(BF16)
    o_ref[...] = (_dot(hb, wm_ref[:, 2 * qk_w + v_w:]) + bm_ref[:, 2 * qk_w + v_w:]).astype(BF16)

    zs = _dot(hb, ws_ref[...]) + bs_ref[...]
    g = zs[:, _SM_GATE:_SM_GATE + 128]
    gf = g + fb_ref[...]
    logsig = jnp.minimum(gf, 0.0) - jnp.log1p(jnp.exp(-jnp.abs(gf)))
    lane = lax.broadcasted_iota(jnp.int32, g.shape, 1)
    gate_ref[...] = jnp.where(lane >= ML_HEADS, logsig, g)

    cs = cs_ref[...]
    qln = _rms(zs[:, _SM_QLAT:_SM_QLAT + MLA_Q_RANK], gq_ref[...]).astype(BF16)
    cq = _dot(qln, wuq_ref[...])
    for hd in range(MLA_HEADS):
        blk = cq[:, hd * QK_PAD:(hd + 1) * QK_PAD]
        t = blk[:, MLA_DNOPE:] * cs
        rot = t + pltpu.roll(t, 64, 1)
        qcat_ref[hd, :, 0:MLA_DNOPE] = (blk[:, :MLA_DNOPE] * Q_SCALE).astype(BF16)
        qcat_ref[hd, :, MLA_DNOPE:] = (rot * Q_SCALE).astype(BF16)
    latent = _rms(zs[:, _SM_KVLAT:_SM_KVLAT + MLA_KV_RANK], gkv_ref[...])
    lat_ref[...] = latent
    tk = zs[:, _SM_KPE:_SM_KPE + 128] * cs
    rotk = tk + pltpu.roll(tk, 64, 1)
    kr_ref[...] = rotk[:, 0:MLA_DROPE]
    krz = jnp.where(lane < MLA_DROPE, rotk, 0.0).astype(BF16)
    latb = latent.astype(BF16)
    kn = _dot(latb, wuk_ref[...])
    for hd in range(MLA_HEADS):
        kcat_ref[hd, :, 0:MLA_DNOPE] = kn[:, hd * MLA_DNOPE:(hd + 1) * MLA_DNOPE].astype(BF16)
        kcat_ref[hd, :, MLA_DNOPE:] = krz
        vvt_ref[hd] = _dot_nt(wuvt_ref[hd], latb).astype(BF16)


def _mixer_in(x, mods, cs, w, tile, tiles_per_mod, cs_tiles):
    n, d = x.shape
    nm, _, r, _ = mods.shape
    grid = (n // tile,)
    const = _resident
    tok = lambda width: pl.BlockSpec((tile, width), lambda i: (i, 0))
    head = lambda width: pl.BlockSpec((MLA_HEADS, tile, width), lambda i: (0, i, 0))
    qk_w = ML_HEADS * ML_DQK
    v_w = ML_HEADS * ML_DV
    out_shape = (
        jax.ShapeDtypeStruct((n, qk_w), BF16), jax.ShapeDtypeStruct((n, qk_w), BF16),
        jax.ShapeDtypeStruct((n, v_w), BF16), jax.ShapeDtypeStruct((n, v_w), BF16),
        jax.ShapeDtypeStruct((n, 128), F32),
        jax.ShapeDtypeStruct((n, MLA_KV_RANK), F32), jax.ShapeDtypeStruct((n, MLA_DROPE), F32),
        jax.ShapeDtypeStruct((MLA_HEADS, n, QK_PAD), BF16), jax.ShapeDtypeStruct((MLA_HEADS, n, QK_PAD), BF16),
        jax.ShapeDtypeStruct((MLA_HEADS, MLA_DV, n), BF16),
    )
    out_specs = (tok(qk_w), tok(qk_w), tok(v_w), tok(v_w), tok(128), tok(MLA_KV_RANK), tok(MLA_DROPE),
                 head(QK_PAD), head(QK_PAD), pl.BlockSpec((MLA_HEADS, MLA_DV, tile), lambda i: (0, 0, i)))
    in_specs = [
        tok(d),
        pl.BlockSpec((nm, 1, r, d), lambda i: (0, i // tiles_per_mod, 0, 0)),
        const(w["g_pre_mix"]), const(w["w_main"]), const(w["b_main"]), const(w["w_small"]), const(w["b_small"]),
        const(w["f_bias"]), const(w["g_q"]), const(w["g_kv"]), const(w["w_uq"]), const(w["w_uk"]), const(w["w_uv_t"]),
        pl.BlockSpec((tile, 128), lambda i: (i % cs_tiles, 0)),
    ]
    return pl.pallas_call(
        _mixer_in_body, grid=grid, in_specs=in_specs, out_specs=out_specs, out_shape=out_shape,
        compiler_params=_cparams(("parallel",)), name="mixer_in",
    )(x, mods, w["g_pre_mix"], w["w_main"], w["b_main"], w["w_small"], w["b_small"], w["f_bias"],
      w["g_q"], w["g_kv"], w["w_uq"], w["w_uk"], w["w_uv_t"], cs)


def _mlstm_body(q_ref, k_ref, v_ref, o_ref, gc_ref, gr_ref, c0_ref, n0_ref, m0_ref, gh_ref,
                y_ref, c_ref, n_ref, m_ref, br_ref):
    chunk = q_ref.shape[0]

    @pl.when(pl.program_id(1) == 0)
    def _():
        c_ref[...] = c0_ref[...]
        n_ref[...] = n0_ref[...]
        m_ref[...] = m0_ref[...]

    row = lax.broadcasted_iota(jnp.int32, (chunk, chunk), 0)
    col = lax.broadcasted_iota(jnp.int32, (chunk, chunk), 1)
    causal = col <= row
    tri = causal.astype(BF16)
    tri_t = (row <= col).astype(BF16)

    gc = gc_ref[...]
    gr = gr_ref[0]
    bc_all = sum(_dot(tri, p) for p in _split3(gc))
    br_ref[...] = sum(_dot(p, tri_t) for p in _split3(gr))

    lane = lax.broadcasted_iota(jnp.int32, gc.shape, 1)
    rowv = lax.broadcasted_iota(jnp.int32, (chunk, 1), 0)
    is_last = rowv == chunk - 1

    def pick(a, j):
        return jnp.sum(jnp.where(lane == j, a, 0.0), axis=-1, keepdims=True)

    def last(a):
        return jnp.sum(jnp.where(is_last, a, 0.0), axis=0, keepdims=True)

    for hd in range(ML_HEADS):
        qs = slice(hd * ML_DQK, (hd + 1) * ML_DQK)
        vs = slice(hd * ML_DV, (hd + 1) * ML_DV)
        q = q_ref[:, qs]
        k = k_ref[:, qs]
        v = v_ref[:, vs]
        ig_c = pick(gc, hd)
        b_c = pick(bc_all, ML_HEADS + hd)
        ig_r = gr_ref[0, hd:hd + 1, :]
        b_r = br_ref[ML_HEADS + hd:ML_HEADS + hd + 1, :]
        m_prev = m_ref[0, hd]
        c_prev = c_ref[0, hd]
        n_prev = n_ref[0, hd]

        dmat = jnp.where(causal, b_c - b_r + ig_r, -jnp.inf)
        inter = b_c + m_prev
        m_t = jnp.maximum(inter, jnp.max(dmat, axis=-1, keepdims=True))
        w_intra = jnp.exp(dmat - m_t)
        w_inter = jnp.exp(inter - m_t)
        qk = _dot_nt(q, k) * w_intra
        num = w_inter * _dot(q, c_prev.astype(BF16)) + _dot(qk.astype(BF16), v)
        qn = jnp.sum(q.astype(F32) * n_prev, axis=-1, keepdims=True)
        den = w_inter * qn + jnp.sum(qk, axis=-1, keepdims=True)
        hcur = num / jnp.maximum(jnp.abs(den), jnp.exp(-m_t))
        hn = hcur * lax.rsqrt(jnp.mean(hcur * hcur, axis=-1, keepdims=True) + EPS)
        y_ref[:, vs] = (hn * gh_ref[:, vs] * jax.nn.sigmoid(o_ref[:, vs].astype(F32))).astype(BF16)

        m_new = last(m_t)
        w_prev = jnp.exp(last(inter) - m_new)
        w_end = jnp.exp(last(b_c) - b_c + ig_c - m_new)
        kw = k.astype(F32) * w_end
        c_ref[0, hd] = w_prev * c_prev + _dot(kw.T.astype(BF16), v)
        n_ref[0, hd] = w_prev * n_prev + jnp.sum(kw, axis=0, keepdims=True)
        m_ref[0, hd] = m_new


def _mlstm(q, k, v, o, gates_col, gates_row, c0, n0, m0, g_head, chunk):
    n = q.shape[0]
    b, _, s = gates_row.shape
    nc = s // chunk
    tok = lambda width: pl.BlockSpec((chunk, width), lambda i, c: (i * nc + c, 0))
    state = lambda a: pl.BlockSpec((1,) + a.shape[1:], lambda i, c: (i,) + (0,) * (a.ndim - 1))
    qk_w = ML_HEADS * ML_DQK
    v_w = ML_HEADS * ML_DV
    return pl.pallas_call(
        _mlstm_body,
        grid=(b, nc),
        in_specs=[tok(qk_w), tok(qk_w), tok(v_w), tok(v_w), tok(128),
                  pl.BlockSpec((1, 8, chunk), lambda i, c: (i, 0, c)),
                  state(c0), state(n0), state(m0),
                  pl.BlockSpec((1, v_w), lambda i, c: (0, 0))],
        out_specs=(tok(v_w), state(c0), state(n0), state(m0)),
        out_shape=(jax.ShapeDtypeStruct((n, v_w), BF16),
                   jax.ShapeDtypeStruct(c0.shape, F32), jax.ShapeDtypeStruct(n0.shape, F32),
                   jax.ShapeDtypeStruct(m0.shape, F32)),
        scratch_shapes=[pltpu.VMEM((8, chunk), F32)],
        compiler_params=_cparams(("parallel", "arbitrary")), name="mlstm",
    )(q, k, v, o, gates_col, gates_row, c0, n0, m0, g_head)


def _flash_body(q_ref, k_ref, vt_ref, o_ref, m_sc, l_sc, acc_sc):
    tile = q_ref.shape[1]
    qi = pl.program_id(2)
    m_sc[...] = jnp.full(m_sc.shape, -jnp.inf, F32)
    l_sc[...] = jnp.zeros(l_sc.shape, F32)
    acc_sc[...] = jnp.zeros(acc_sc.shape, F32)

    def update(koff, ksize, q0, masked):
        qs = slice(q0, tile)
        kt = k_ref[0, pl.ds(koff, ksize), :]
        vt = vt_ref[0, :, pl.ds(koff, ksize)]
        st = _dot_nt(kt, q_ref[0, qs, :])
        if masked:
            shift = STREAM_CHUNK.bit_length() - 1
            kc = lax.broadcasted_iota(jnp.int32, st.shape, 0) >> shift
            qc = lax.broadcasted_iota(jnp.int32, st.shape, 1) >> shift
            st = jnp.where(kc <= qc, st, -jnp.inf)
        m_old = m_sc[:, qs]
        m_new = jnp.maximum(m_old, jnp.max(st, axis=0, keepdims=True))
        alpha = jnp.exp2(m_old - m_new)
        p = jnp.exp2(st - m_new)
        l_sc[:, qs] = alpha * l_sc[:, qs] + jnp.sum(p, axis=0, keepdims=True)
        acc_sc[:, qs] = alpha * acc_sc[:, qs] + _dot(vt, p.astype(BF16))
        m_sc[:, qs] = m_new

    def step(j, masked):
        off = pl.multiple_of(j * tile, tile)
        if not masked:
            update(off, tile, 0, False)
        else:
            for k0 in range(0, tile, FLASH_DIAG_CHUNK):
                update(off + k0, FLASH_DIAG_CHUNK, k0, True)

    def body(i, carry):
        for u in range(FLASH_UNROLL):
            step(FLASH_UNROLL * i + u, False)
        return carry

    def rest(j, carry):
        step(j, False)
        return carry

    n_full = qi // FLASH_UNROLL
    lax.fori_loop(0, n_full, body, 0)
    lax.fori_loop(n_full * FLASH_UNROLL, qi, rest, 0)
    step(qi, True)
    o_ref[...] = (acc_sc[...] / l_sc[...]).T.astype(BF16)


def _flash(qcat, kcat, vvt, batch, seq, tile):
    heads, n, _ = qcat.shape
    nq = seq // tile
    return pl.pallas_call(
        _flash_body,
        grid=(batch, heads, nq),
        in_specs=[pl.BlockSpec((1, tile, QK_PAD), lambda b, h, i: (h, b * nq + i, 0)),
                  pl.BlockSpec((1, seq, QK_PAD), lambda b, h, i: (h, b, 0)),
                  pl.BlockSpec((1, MLA_DV, seq), lambda b, h, i: (h, 0, b))],
        out_specs=pl.BlockSpec((tile, MLA_DV), lambda b, h, i: (b * nq + i, h)),
        out_shape=jax.ShapeDtypeStruct((n, heads * MLA_DV), BF16),
        scratch_shapes=[pltpu.VMEM((1, tile), F32), pltpu.VMEM((1, tile), F32), pltpu.VMEM((MLA_DV, tile), F32)],
        compiler_params=_cparams(("parallel", "parallel", "arbitrary")),
        name="flash",
    )(qcat, kcat, vvt)


def _sattn_body(q_ref, clat_ref, nlat_ref, ckr_ref, nk_ref, wuk_ref, wuv_ref, y_ref, qa_sc, qr_sc):
    s_new = nlat_ref.shape[0]
    for hd in range(MLA_HEADS):
        qh = q_ref[hd]
        hs = slice(hd * MLA_DNOPE, (hd + 1) * MLA_DNOPE)
        qa_sc[hd * s_new:(hd + 1) * s_new, :] = _dot_nt(qh[:, 0:MLA_DNOPE], wuk_ref[:, hs]).astype(BF16)
        qr_sc[hd * s_new:(hd + 1) * s_new, :] = qh[:, MLA_DNOPE:]
    qa = qa_sc[...]
    qr = qr_sc[...]
    clat = clat_ref[0].astype(BF16)
    nlat = nlat_ref[...].astype(BF16)
    ckr = ckr_ref[0].astype(BF16)
    nkr = nk_ref[0, :, MLA_DNOPE:]
    s_c = _dot_nt(qa, clat) + _dot_nt(qr, ckr)
    s_n = _dot_nt(qa, nlat) + _dot_nt(qr, nkr)
    m = jnp.maximum(jnp.max(s_c, axis=-1, keepdims=True), jnp.max(s_n, axis=-1, keepdims=True))
    p_c = jnp.exp2(s_c - m)
    p_n = jnp.exp2(s_n - m)
    l = jnp.sum(p_c, axis=-1, keepdims=True) + jnp.sum(p_n, axis=-1, keepdims=True)
    o_lat = ((_dot(p_c.astype(BF16), clat) + _dot(p_n.astype(BF16), nlat)) / l).astype(BF16)
    for hd in range(MLA_HEADS):
        hs = slice(hd * MLA_DV, (hd + 1) * MLA_DV)
        y_ref[:, hs] = _dot(o_lat[hd * s_new:(hd + 1) * s_new, :], wuv_ref[:, hs]).astype(BF16)


def _sattn(qcat, cache_lat, new_lat, cache_kr_pad, kcat, w_uk, w_uv, s_new):
    heads, n, _ = qcat.shape
    b, past, _ = cache_lat.shape
    const = _resident
    return pl.pallas_call(
        _sattn_body,
        grid=(b,),
        in_specs=[pl.BlockSpec((heads, s_new, QK_PAD), lambda i: (0, i, 0)),
                  pl.BlockSpec((1, past, MLA_KV_RANK), lambda i: (i, 0, 0)),
                  pl.BlockSpec((s_new, MLA_KV_RANK), lambda i: (i, 0)),
                  pl.BlockSpec((1, past, 128), lambda i: (i, 0, 0)),
                  pl.BlockSpec((1, s_new, QK_PAD), lambda i: (0, i, 0)),
                  const(w_uk), const(w_uv)],
        out_specs=pl.BlockSpec((s_new, heads * MLA_DV), lambda i: (i, 0)),
        out_shape=jax.ShapeDtypeStruct((n, heads * MLA_DV), BF16),
        scratch_shapes=[pltpu.VMEM((heads * s_new, MLA_KV_RANK), BF16), pltpu.VMEM((heads * s_new, 128), BF16)],
        compiler_params=_cparams(("parallel",)), name="sattn",
    )(qcat, cache_lat, new_lat, cache_kr_pad, kcat, w_uk, w_uv)


def _tail_body(x_ref, yml_ref, ymla_ref, mod_ref, gpre_ref, gpost_ref, gffn_ref, wg_ref, bg_ref,
               wml_ref, wmla_ref, wout_ref, wr_ref, br_ref,
               x1_ref, h2_ref, te_ref, tw_ref):
    x = x_ref[...]
    d = x.shape[1]
    sh1, sc1, gt1, sh2, sc2 = (mod_ref[j, 0] for j in range(5))
    hb = (_rms(x, gpre_ref[...]) * (1.0 + sc1) + sh1).astype(BF16)
    g = jax.nn.sigmoid(_dot(hb, wg_ref[...]) + bg_ref[...])
    yb = g[:, :d] * _dot(yml_ref[...], wml_ref[...]) + g[:, d:] * _dot(ymla_ref[...], wmla_ref[...])
    y = _dot(yb.astype(BF16), wout_ref[...])
    x1 = x + gt1 * _rms(y, gpost_ref[...])
    x1_ref[...] = x1
    h2 = _rms(x1, gffn_ref[...]) * (1.0 + sc2) + sh2
    _store_token_tiles(h2_ref, h2)
    hi = h2.astype(BF16)
    lo = (h2 - hi.astype(F32)).astype(BF16)
    both = _dot(hi, wr_ref[...])
    logits = both[:, :128] + both[:, 128:] + _dot(lo, wr_ref[:, 0:128]) + br_ref[...]
    lane = lax.broadcasted_iota(jnp.int32, logits.shape, 1)
    lane_f = lane.astype(F32)
    lg = jnp.where(lane < N_EXPERTS, logits, -jnp.inf)
    te = jnp.zeros(lg.shape, F32)
    tw = jnp.zeros(lg.shape, F32)
    top0 = None
    for kk in range(TOP_K):
        mx = jnp.max(lg, axis=-1, keepdims=True)
        idx = jnp.min(jnp.where(lg == mx, lane_f, 128.0), axis=-1, keepdims=True)
        if kk == 0:
            top0 = mx
        te = jnp.where(lane == kk, idx, te)
        tw = jnp.where(lane == kk, jnp.exp(mx - top0), tw)
        lg = jnp.where(lane_f == idx, -jnp.inf, lg)
    te_ref[...] = te.astype(jnp.int32)
    tw_ref[...] = tw / jnp.sum(tw, axis=-1, keepdims=True)


def _tail(x, yml, ymla, mods, w, tile, tiles_per_mod, n_total, row0=0, into=None):
    n, d = x.shape
    nm, _, r, _ = mods.shape
    assert nm == 5 and row0 % tile == 0
    t0 = row0 // tile
    const = _resident
    tok = lambda width: pl.BlockSpec((tile, width), lambda i: (i, 0))
    out = lambda rows, width: pl.BlockSpec((rows, width), lambda i: (t0 + i, 0))
    names = ("g_pre_mix", "g_post_mix", "g_pre_ffn", "w_gate", "b_gate", "w_br_ml", "w_br_mla", "w_out",
             "w_router_hilo", "b_router")
    n_in = 4 + len(names)
    extra = () if into is None else tuple(into)

    def body(*refs):
        _tail_body(*refs[:n_in], *refs[n_in + len(extra):])

    return pl.pallas_call(
        body,
        grid=(n // tile,),
        in_specs=[tok(d), tok(d), tok(d),
                  pl.BlockSpec((5, 1, r, d), lambda i: (0, i // tiles_per_mod, 0, 0))]
                 + [const(w[k]) for k in names] + [pl.BlockSpec(memory_space=pl.ANY)] * len(extra),
        out_specs=(out(tile, d), out(tile * TOKEN_TILE_ROWS, 128), out(tile, 128), out(tile, 128)),
        out_shape=(jax.ShapeDtypeStruct((n_total, d), F32),
                   jax.ShapeDtypeStruct((n_total * TOKEN_TILE_ROWS, 128), F32),
                   jax.ShapeDtypeStruct((n_total, 128), jnp.int32), jax.ShapeDtypeStruct((n_total, 128), F32)),
        input_output_aliases={n_in + j: j for j in range(len(extra))},
        compiler_params=_cparams(("parallel",)), name="tail",
    )(x, yml, ymla, mods, *[w[k] for k in names], *extra)


def _moe_body(be_ref, jb_ref, nu_ref, src_ref, srcn_ref, dstp_ref, dst_ref, h2_hbm, wup_ref, bup_ref, wdn_ref,
              bdn_ref, out_hbm, xbuf0, xbuf1, obuf0, obuf1, wup_bf, wdn_bf, sem_in, sem_out):
    del jb_ref
    w = pl.program_id(0)
    n_used = nu_ref[0]
    rows = src_ref.shape[2]
    tr = TOKEN_TILE_ROWS
    ff = wdn_ref.shape[1]

    new_expert = jnp.logical_or(w == 0, be_ref[w] != be_ref[jnp.maximum(w - 1, 0)])

    @pl.when(jnp.logical_and(w < n_used, new_expert))
    def _():
        wup_bf[...] = wup_ref[0].astype(BF16)
        wdn_bf[...] = wdn_ref[0].astype(BF16)

    xbufs, obufs = (xbuf0, xbuf1), (obuf0, obuf1)

    def gather_copy(idx_ref, slot, i):
        return pltpu.make_async_copy(h2_hbm.at[pl.ds(pl.multiple_of(idx_ref[0, 0, i], tr), tr)],
                                     xbufs[slot].at[pl.ds(i * tr, tr)], sem_in.at[slot])

    def scatter_copy(idx_ref, slot, i):
        return pltpu.make_async_copy(obufs[slot].at[pl.ds(i * tr, tr)],
                                     out_hbm.at[pl.ds(pl.multiple_of(idx_ref[0, 0, i], tr), tr)], sem_out)

    def gather_wait(slot):
        pltpu.make_async_copy(h2_hbm.at[pl.ds(0, rows * tr)], xbufs[slot], sem_in.at[slot]).wait()

    def scatter_wait(slot):
        pltpu.make_async_copy(obufs[slot], out_hbm.at[pl.ds(0, rows * tr)], sem_out).wait()

    def rolled(copy, idx_ref, slot):
        def issue(i, c):
            copy(idx_ref, slot, i).start()
            return c
        lax.fori_loop(0, rows, issue, 0, unroll=8)

    def item(slot):
        other = 1 - slot
        @pl.when(w == 0)
        def _():
            rolled(gather_copy, src_ref, slot)
            obufs[other][...] = jnp.zeros(obufs[other].shape, F32)

        gather_wait(slot)

        @pl.when(w > 0)
        def _():
            scatter_wait(slot)

        x = jnp.concatenate(_load_token_tiles(xbufs[slot], rows), axis=-1).astype(BF16)
        for i in range(rows):
            gather_copy(srcn_ref, other, i).start()
            scatter_copy(dstp_ref, other, i).start()
        gu = _dot(x, wup_bf[...]) + bup_ref[0]
        gl = jnp.minimum(gu[:, :ff], SWIGLU_LIMIT)
        up = jnp.clip(gu[:, ff:], -SWIGLU_LIMIT, SWIGLU_LIMIT)
        act = (up + 1.0) * gl * jax.nn.sigmoid(SWIGLU_ALPHA * gl)
        _store_token_tiles(obufs[slot], _dot(act.astype(BF16), wdn_bf[...]) + bdn_ref[0])

        @pl.when(w == n_used - 1)
        def _():
            gather_wait(other)
            scatter_wait(other)
            rolled(scatter_copy, dst_ref, slot)
            scatter_wait(slot)

    for slot in (1, 0):
        pl.when(jnp.logical_and(w < n_used, (w & 1) == slot))(functools.partial(item, slot))


def _moe(item_e, item_blk, n_used, src_blk, dst_shift, h2t, w_up, b_up, w_down, b_down, out_rows):
    n_items = dst_shift.shape[0] - 1
    d = w_up.shape[1]
    ff2 = w_up.shape[2]
    ff = w_down.shape[1]
    tr = TOKEN_TILE_ROWS
    smem = lambda imap: pl.BlockSpec((1, 1, MOE_ROWS), imap, memory_space=pltpu.SMEM)
    wspec = lambda shape: pl.BlockSpec(shape, lambda w, be, jb, nu: (be[w], 0, 0))
    grid_spec = pltpu.PrefetchScalarGridSpec(
        num_scalar_prefetch=3,
        grid=(n_items,),
        in_specs=[smem(lambda w, be, jb, nu: (jb[w], 0, 0)),
                  smem(lambda w, be, jb, nu: (jb[jnp.minimum(w + 1, n_items - 1)], 0, 0)),
                  smem(lambda w, be, jb, nu: (w, 0, 0)),
                  smem(lambda w, be, jb, nu: (w + 1, 0, 0)),
                  pl.BlockSpec(memory_space=pl.ANY),
                  wspec((1, d, ff2)), wspec((1, 1, ff2)), wspec((1, ff, d)), wspec((1, 1, d))],
        out_specs=pl.BlockSpec(memory_space=pl.ANY),
        scratch_shapes=[pltpu.VMEM((MOE_ROWS * tr, 128), F32)] * 4
                       + [pltpu.VMEM((d, ff2), BF16), pltpu.VMEM((ff, d), BF16),
                          pltpu.SemaphoreType.DMA((2,)), pltpu.SemaphoreType.DMA],
    )
    return pl.pallas_call(
        _moe_body,
        grid_spec=grid_spec,
        out_shape=jax.ShapeDtypeStruct((out_rows * tr, 128), F32),
        compiler_params=_cparams(("arbitrary",)),
        name="moe",
    )(item_e, item_blk, n_used, src_blk, src_blk, dst_shift, dst_shift, h2t, w_up, b_up, w_down, b_down)


def _combine_body(o0_ref, o1_ref, o2_ref, o3_ref, tw_ref, x1_ref, mod_ref, g_ref, y_ref):
    tw = tw_ref[...]
    t = tw.shape[0]
    lane = lax.broadcasted_iota(jnp.int32, tw.shape, 1)
    acc = None
    for kk, o_ref in enumerate((o0_ref, o1_ref, o2_ref, o3_ref)):
        wk = jnp.sum(jnp.where(lane == kk, tw, 0.0), axis=-1, keepdims=True)
        terms = [p * wk for p in _load_token_tiles(o_ref, t)]
        acc = terms if acc is None else [a + b for a, b in zip(acc, terms)]
    d = x1_ref.shape[1]
    ms = sum(jnp.sum(a * a, axis=-1, keepdims=True) for a in acc) * (1.0 / d)
    inv = lax.rsqrt(ms + EPS)
    for s, a in enumerate(acc):
        ls = slice(s * 128, (s + 1) * 128)
        y_ref[:, ls] = x1_ref[:, ls] + mod_ref[0, 0, :, ls] * (a * inv * g_ref[:, ls])


def _combine(moe_out, tw, x1, mods, g_post_ffn, n_pad, row0, rows, tile, tiles_per_mod):
    d = x1.shape[1]
    r = mods.shape[2]
    assert row0 % tile == 0 and n_pad % tile == 0 and rows % tile == 0
    t0 = row0 // tile
    kt = n_pad // tile
    ospec = lambda kk: pl.BlockSpec((tile * TOKEN_TILE_ROWS, 128), lambda i: (kk * kt + t0 + i, 0))
    return pl.pallas_call(
        _combine_body,
        grid=(rows // tile,),
        in_specs=[ospec(0), ospec(1), ospec(2), ospec(3),
                  pl.BlockSpec((tile, 128), lambda i: (t0 + i, 0)),
                  pl.BlockSpec((tile, d), lambda i: (t0 + i, 0)),
                  pl.BlockSpec((1, 1, r, d), lambda i: (0, i // tiles_per_mod, 0, 0)),
                  pl.BlockSpec((1, d), lambda i: (0, 0))],
        out_specs=pl.BlockSpec((tile, d), lambda i: (i, 0)),
        out_shape=jax.ShapeDtypeStruct((rows, d), F32),
        compiler_params=_cparams(("parallel",)), name="combine",
    )(moe_out, moe_out, moe_out, moe_out, tw, x1, mods, g_post_ffn)


def _rope_table(pos):
    inv = 1.0 / (ROPE_THETA ** (jnp.arange(0, MLA_DROPE, 2, dtype=F32) / MLA_DROPE))
    ang = pos.astype(F32)[:, None] * inv[None, :]
    c, s = jnp.cos(ang), jnp.sin(ang)
    return jnp.concatenate([c, c, s, s], axis=-1)


def _rotate_cols(w):
    half = w.shape[-1] // 2
    return jnp.concatenate([-w[..., half:], w[..., :half]], axis=-1)


def _prep_weights(w_in, b_in, ml_f_bias, mla_g_q, w_uq, mla_g_kv, w_uk, w_uv, g_pre_mix):
    qk_w = ML_HEADS * ML_DQK
    v_w = ML_HEADS * ML_DV
    sizes = [qk_w, qk_w, v_w, ML_HEADS, ML_HEADS, v_w, MLA_Q_RANK, MLA_KV_RANK, MLA_DROPE]
    pts = [int(p) for p in np.cumsum(sizes)[:-1]]
    wq, wk, wv, wi, wf, wo, wql, wkv, wpe = jnp.split(w_in, pts, axis=-1)
    bq, bk, bv, bi, bf, bo, bql, bkv, bpe = jnp.split(b_in, pts, axis=-1)
    d = w_in.shape[0]
    pad_w = jnp.zeros((d, 128 - 2 * ML_HEADS), F32)
    pad_b = jnp.zeros((128 - 2 * ML_HEADS,), F32)
    w_small = jnp.concatenate([wql, wkv, wpe, _rotate_cols(wpe), wi, wf, pad_w], axis=-1)
    b_small = jnp.concatenate([bql, bkv, bpe, _rotate_cols(bpe), bi, bf, pad_b], axis=-1)
    f_bias = jnp.concatenate([jnp.zeros((ML_HEADS,), F32), ml_f_bias, pad_b], axis=-1)
    wuq_h = w_uq.reshape(MLA_Q_RANK, MLA_HEADS, MLA_DNOPE + MLA_DROPE)
    wuq_r = wuq_h[..., MLA_DNOPE:]
    wuq_ext = jnp.concatenate([wuq_h, _rotate_cols(wuq_r)], axis=-1).reshape(MLA_Q_RANK, MLA_HEADS * QK_PAD)
    return {
        "g_pre_mix": g_pre_mix[None, :],
        "w_main": jnp.concatenate([wq, wk, wv, wo], axis=-1).astype(BF16),
        "b_main": jnp.concatenate([bq, bk, bv, bo], axis=-1)[None, :],
        "w_small": w_small.astype(BF16), "b_small": b_small[None, :], "f_bias": f_bias[None, :],
        "g_q": mla_g_q[None, :], "g_kv": mla_g_kv[None, :],
        "w_uq": wuq_ext.astype(BF16), "w_uk": w_uk.astype(BF16), "w_uv": w_uv.astype(BF16),
        "w_uv_t": w_uv.reshape(MLA_KV_RANK, MLA_HEADS, MLA_DV).transpose(1, 2, 0).astype(BF16),
    }


def _route_items(te, n_tok, n_pad):
    n_assign = n_tok * TOP_K
    n_blocks = -(-n_assign // MOE_ROWS)
    n_items = n_blocks + N_EXPERTS - 1
    i32 = jnp.int32
    e_flat = te.reshape(-1)
    skey = jnp.sort(e_flat * (1 << ORDER_BITS) + jnp.arange(n_assign, dtype=i32))
    order = skey & ((1 << ORDER_BITS) - 1)
    start = jnp.sum(e_flat[None, :] < jnp.arange(N_EXPERTS + 1, dtype=i32)[:, None], axis=1).astype(i32)
    lo_e, hi_e = start[:-1], start[1:]
    first_blk = lo_e // MOE_ROWS
    n_it = jnp.where(hi_e > lo_e, (hi_e - 1) // MOE_ROWS - first_blk + 1, 0)
    it_end = jnp.cumsum(n_it)
    n_used = it_end[-1]
    w = jnp.minimum(jnp.arange(n_items, dtype=i32), n_used - 1)
    item_e = jnp.minimum(jnp.sum(it_end[None, :] <= w[:, None], axis=1), N_EXPERTS - 1).astype(i32)
    onehot = (item_e[:, None] == jnp.arange(N_EXPERTS, dtype=i32)[None, :]).astype(i32)
    per_item = lambda table: jnp.sum(onehot * table[None, :], axis=1)
    item_blk = per_item(first_blk) + w - per_item(it_end - n_it)
    r = jnp.arange(MOE_ROWS, dtype=i32)[None, :]
    pos = item_blk[:, None] * MOE_ROWS + r
    mine = jnp.logical_and(pos >= per_item(lo_e)[:, None], pos < per_item(hi_e)[:, None])
    order_p = jnp.pad(order, (0, n_blocks * MOE_ROWS - n_assign)).reshape(n_blocks, MOE_ROWS)
    src_blk = order_p >> 2
    dst_blk = (order_p & 3) * n_pad + src_blk
    spare = TOP_K * n_pad + r
    dst_item = jnp.where(mine, dst_blk[item_blk], spare)
    dst_shift = jnp.concatenate([spare, dst_item], axis=0)
    tr = TOKEN_TILE_ROWS
    return item_e, item_blk, n_used.reshape(1).astype(i32), (src_blk * tr)[:, None, :], (dst_shift * tr)[:, None, :]


def kernel(x_prompt, x_sample, cache_kv_latent, cache_k_rope, state_mlstm_C, state_mlstm_n, state_mlstm_m,
           c_prompt, c_sample, w_ada, b_ada, g_pre_mix, g_post_mix, g_pre_ffn, g_post_ffn, w_in, b_in,
           ml_f_bias, ml_head_g, mla_g_q, w_uq, mla_g_kv, w_uk, w_uv, w_br_ml, w_br_mla, w_gate, b_gate,
           w_out, w_router, b_router, w_up, b_up, w_down, b_down):
    assert w_ada.shape[0] == 1, "single-layer step"
    bp, sp, d = x_prompt.shape
    bs, ss, _ = x_sample.shape
    past = cache_kv_latent.shape[2]
    n_p, n_s = bp * sp, bs * ss
    n_tok = n_p + n_s
    tile_p = TOK_TILE
    tile_s = min(TOK_TILE, n_s)
    n_pad = -(-n_tok // tile_p) * tile_p
    assert n_tok * TOP_K < (1 << ORDER_BITS) and d == TOKEN_TILE_ROWS * 128
    assert sp % ATT_TILE == 0 and sp % ML_CHUNK == 0 and sp % tile_p == 0
    assert n_s % tile_s == 0 and n_p % tile_s == 0 and n_pad % tile_s == 0

    mods = _ada(jnp.concatenate([c_prompt, c_sample], axis=0), w_ada[0], b_ada[0][None, :])
    mods = mods.reshape(bp + bs, 6, d).transpose(1, 0, 2)
    mods_p = mods[:, :bp][:, :, None, :]
    mods_s = jnp.broadcast_to(mods[:, bp:, None, :], (6, bs, ss, d)).reshape(6, n_s // tile_s, tile_s, d)

    wm = _prep_weights(w_in[0], b_in[0], ml_f_bias[0], mla_g_q[0], w_uq[0], mla_g_kv[0], w_uk[0], w_uv[0],
                       g_pre_mix[0])
    cs_p = _rope_table(jnp.arange(sp))
    cs_s = jnp.tile(_rope_table(past + jnp.arange(ss)), (bs, 1))

    xp = x_prompt.reshape(n_p, d)
    xs = x_sample.reshape(n_s, d)
    tiles_per_stream = sp // tile_p
    mp = _mixer_in(xp, mods_p[:2], cs_p, wm, tile_p, tiles_per_stream, tiles_per_stream)
    ms = _mixer_in(xs, mods_s[:2], cs_s, wm, tile_s, 1, n_s // tile_s)

    g_head = ml_head_g[0][None, :]

    def gate_rows(g, b, s):
        return g[:, :2 * ML_HEADS].reshape(b, s, 2 * ML_HEADS).transpose(0, 2, 1)

    qp, kp, vp, op, gp, lat_p, kr_p, qcat_p, kcat_p, vv_p = mp
    zc = jnp.zeros((bp, ML_HEADS, ML_DQK, ML_DV), F32)
    zn = jnp.zeros((bp, ML_HEADS, 1, ML_DQK), F32)
    zm = jnp.zeros((bp, ML_HEADS, 1, 1), F32)
    yml_p, c_p, nn_p, m_p = _mlstm(qp, kp, vp, op, gp, gate_rows(gp, bp, sp), zc, zn, zm, g_head, ML_CHUNK)
    ymla_p = _flash(qcat_p, kcat_p, vv_p, bp, sp, ATT_TILE)

    qs, ks, vs, os_, gs, lat_s, kr_s, qcat_s, kcat_s, _ = ms
    yml_s, c_s, nn_s, m_s = _mlstm(qs, ks, vs, os_, gs, gate_rows(gs, bs, ss),
                                   state_mlstm_C[0], state_mlstm_n[0][:, :, None, :],
                                   state_mlstm_m[0][:, :, None, None], g_head, ss)
    ckr_pad = jnp.pad(cache_k_rope[0], ((0, 0), (0, 0), (0, 128 - MLA_DROPE)))
    ymla_s = _sattn(qcat_s, cache_kv_latent[0], lat_s, ckr_pad, kcat_s, wm["w_uk"], wm["w_uv"], ss)

    w_r = jnp.pad(w_router[0], ((0, 0), (0, 128 - N_EXPERTS)))
    w_r_hi = w_r.astype(BF16)
    wt = {
        "g_pre_mix": g_pre_mix[0][None, :], "g_post_mix": g_post_mix[0][None, :], "g_pre_ffn": g_pre_ffn[0][None, :],
        "w_gate": w_gate[0].astype(BF16), "b_gate": b_gate[0][None, :],
        "w_br_ml": w_br_ml[0].astype(BF16), "w_br_mla": w_br_mla[0].astype(BF16), "w_out": w_out[0].astype(BF16),
        "w_router_hilo": jnp.concatenate([w_r_hi, (w_r - w_r_hi.astype(F32)).astype(BF16)], axis=1),
        "b_router": jnp.pad(b_router[0], (0, 128 - N_EXPERTS))[None, :],
    }
    merged = _tail(xp, yml_p, ymla_p, mods_p[:5], wt, tile_p, tiles_per_stream, n_tok)
    x1, h2, te, tw = _tail(xs, yml_s, ymla_s, mods_s[:5], wt, tile_s, 1, n_tok, row0=n_p, into=merged)
    te = te[:, :TOP_K]

    item_e, item_blk, n_used, src_blk, dst_shift = _route_items(te, n_tok, n_pad)
    moe_out = _moe(item_e, item_blk, n_used, src_blk, dst_shift, h2, w_up[0], b_up[0][:, None, :],
                   w_down[0], b_down[0][:, None, :], n_pad * TOP_K + MOE_ROWS)
    gff = g_post_ffn[0][None, :]
    y_p = _combine(moe_out, tw, x1, mods_p[5:], gff, n_pad, 0, n_p, tile_p, tiles_per_stream)
    y_s = _combine(moe_out, tw, x1, mods_s[5:], gff, n_pad, n_p, n_s, tile_s, 1)

    return (y_p.reshape(bp, sp, d), y_s.reshape(bs, ss, d),
            lat_p.reshape(1, bp, sp, MLA_KV_RANK), kr_p.reshape(1, bp, sp, MLA_DROPE),
            c_p[None], nn_p.reshape(1, bp, ML_HEADS, ML_DQK), m_p.reshape(1, bp, ML_HEADS),
            lat_s.reshape(1, bs, ss, MLA_KV_RANK), kr_s.reshape(1, bs, ss, MLA_DROPE),
            c_s[None], nn_s.reshape(1, bs, ML_HEADS, ML_DQK), m_s.reshape(1, bs, ML_HEADS))
```
